```python
import jax, jax.numpy as jnp
from jax import lax
import numpy as np

D_MODEL = 1024
BATCH = 1
SEQ = 16384
DEPTH = 4

GRID_W = 64
CTX_LEN = 256
Q_BLOCK = 128
ROPE_THETA = 10000.0
EPS = 1e-6
N_MOD = 6
N_BRANCH = 3

CONV_DIM = 512
CONV_WIDTH = 31
MLA_HEADS = 8
MLA_Q_RANK = 384
MLA_KV_RANK = 256
MLA_NOPE = 64
MLA_ROPE = 32
MLA_V = 64
MLA_QK = MLA_NOPE + MLA_ROPE
GQA_HEADS = 8
GQA_KV_HEADS = 2
GQA_HEAD_DIM = 64
GQA_GROUP = GQA_HEADS // GQA_KV_HEADS
D_FF = 2816
FFN_CONV_WIDTH = 3

MLA_SCALE = MLA_QK ** -0.5
GQA_SCALE = GQA_HEAD_DIM ** -0.5

SECTION_WIDTHS = (2 * CONV_DIM, MLA_Q_RANK, MLA_KV_RANK, MLA_ROPE,
                  GQA_HEADS * GQA_HEAD_DIM, GQA_KV_HEADS * GQA_HEAD_DIM, GQA_KV_HEADS * GQA_HEAD_DIM,
                  N_BRANCH * D_MODEL)
IN_COLS = 2 * CONV_DIM + MLA_Q_RANK + MLA_KV_RANK + MLA_ROPE + (GQA_HEADS + 2 * GQA_KV_HEADS) * GQA_HEAD_DIM + N_BRANCH * D_MODEL

kernel_name = 'hybrid_conv_mla_gqa_dit_trunk'


def rms_norm(x, g):
    xf = x.astype(jnp.float32)
    y = xf * lax.rsqrt(jnp.mean(xf * xf, axis=-1, keepdims=True) + EPS)
    return (y * g.astype(jnp.float32)).astype(x.dtype)


def layer_norm(x, g, b):
    xf = x.astype(jnp.float32)
    mu = jnp.mean(xf, axis=-1, keepdims=True)
    xc = xf - mu
    var = jnp.mean(xc * xc, axis=-1, keepdims=True)
    return (xc * lax.rsqrt(var + EPS) * g.astype(jnp.float32) + b.astype(jnp.float32)).astype(x.dtype)


def modulate(h, shift, scale):
    return h * (1.0 + scale) + shift


def adaln(cond, w_mod, b_mod):
    m = jax.nn.silu(cond) @ w_mod + b_mod
    return [t[:, None, :] for t in jnp.split(m, N_MOD, axis=-1)]


def split_sections(z):
    pts, acc = [], 0
    for w in SECTION_WIDTHS[:-1]:
        acc += w
        pts.append(acc)
    return jnp.split(z, pts, axis=-1)


def axial_rope_tables(row_idx, col_idx, rot_dim):
    n_freq = rot_dim // 4
    inv = 1.0 / (ROPE_THETA ** (jnp.arange(n_freq, dtype=jnp.float32) / n_freq))
    ang_r = row_idx.astype(jnp.float32)[:, None] * inv
    ang_c = col_idx.astype(jnp.float32)[:, None] * inv
    ang = jnp.concatenate([ang_r, ang_r, ang_c, ang_c], axis=-1)
    return jnp.cos(ang), jnp.sin(ang)


def apply_axial_rope(x, cos, sin):
    x1, x2, x3, x4 = jnp.split(x, 4, axis=-1)
    rot = jnp.concatenate([-x2, x1, -x4, x3], axis=-1)
    return (x * cos[:, None, :] + rot * sin[:, None, :]).astype(x.dtype)


def depthwise_conv(x, w, b):
    k, ch = w.shape
    y = lax.conv_general_dilated(x, w[:, None, :], window_strides=(1,), padding=[(k // 2, k // 2)],
                                 dimension_numbers=('NWC', 'WIO', 'NWC'), feature_group_count=ch)
    return y + b


def attention_block(q, k, v, scale):
    s = jnp.einsum('bqhgd,bkhd->bhgqk', q, k).astype(jnp.float32) * scale
    p = jax.nn.softmax(s, axis=-1)
    return jnp.einsum('bhgqk,bkhd->bqhgd', p.astype(v.dtype), v)


def blocked_attention(q, k, v, scale):
    b, n, hkv, g, dk = q.shape
    nb = n // Q_BLOCK
    qb = q.reshape(b, nb, Q_BLOCK, hkv, g, dk).transpose(1, 0, 2, 3, 4, 5)
    out = lax.map(lambda qi: attention_block(qi, k, v, scale), qb)
    return out.transpose(1, 0, 2, 3, 4, 5).reshape(b, n, hkv, g, -1)


def project_tokens(h, p, rope_m, rope_g):
    b, n, _ = h.shape
    z = h @ p['w_in']
    z_glu, z_qa, z_kva, z_kr, z_gq, z_gk, z_gv, z_gate = split_sections(z)
    q_m = (rms_norm(z_qa, p['g_q_a']) @ p['w_q_b']).reshape(b, n, MLA_HEADS, MLA_QK)
    kv = (rms_norm(z_kva, p['g_kv_a']) @ p['w_kv_b']).reshape(b, n, MLA_HEADS, MLA_NOPE + MLA_V)
    k_nope, v_m = kv[..., :MLA_NOPE], kv[..., MLA_NOPE:]
    k_rope = jnp.broadcast_to(z_kr[:, :, None, :], (b, n, MLA_HEADS, MLA_ROPE))
    k_m = jnp.concatenate([k_nope, k_rope], axis=-1)
    q_m = rms_norm(q_m, p['g_mla_q'])
    k_m = rms_norm(k_m, p['g_mla_k'])
    if rope_m is not None:
        cos, sin = rope_m
        q_m = jnp.concatenate([q_m[..., :MLA_NOPE], apply_axial_rope(q_m[..., MLA_NOPE:], cos, sin)], axis=-1)
        k_m = jnp.concatenate([k_m[..., :MLA_NOPE], apply_axial_rope(k_m[..., MLA_NOPE:], cos, sin)], axis=-1)
    q_g = rms_norm(z_gq.reshape(b, n, GQA_HEADS, GQA_HEAD_DIM), p['g_gqa_q'])
    k_g = rms_norm(z_gk.reshape(b, n, GQA_KV_HEADS, GQA_HEAD_DIM), p['g_gqa_k'])
    v_g = z_gv.reshape(b, n, GQA_KV_HEADS, GQA_HEAD_DIM)
    if rope_g is not None:
        cos, sin = rope_g
        q_g = apply_axial_rope(q_g, cos, sin)
        k_g = apply_axial_rope(k_g, cos, sin)
    q_m = q_m[:, :, :, None, :]
    q_g = q_g.reshape(b, n, GQA_KV_HEADS, GQA_GROUP, GQA_HEAD_DIM)
    return z_glu, z_gate, (q_m, k_m, v_m), (q_g, k_g, v_g)


def conformer_conv(z_glu, p):
    a, g = jnp.split(z_glu, 2, axis=-1)
    y = a * jax.nn.sigmoid(g)
    y = depthwise_conv(y, p['conv_dw_w'], p['conv_dw_b'])
    y = jax.nn.silu(layer_norm(y, p['conv_ln_g'], p['conv_ln_b']))
    return y @ p['w_conv_out']


def merge_branches(z_glu, z_gate, o_m, o_g, p):
    b, n, _ = z_gate.shape
    br_conv = conformer_conv(z_glu, p)
    br_mla = o_m.reshape(b, n, MLA_HEADS * MLA_V) @ p['w_mla_o']
    br_gqa = o_g.reshape(b, n, GQA_HEADS * GQA_HEAD_DIM) @ p['w_gqa_o']
    gates = jax.nn.sigmoid((z_gate + p['b_gate']).astype(jnp.float32)).astype(z_gate.dtype)
    gates = gates.reshape(b, n, N_BRANCH, D_MODEL)
    merged = gates[:, :, 0] * br_conv + gates[:, :, 1] * br_mla + gates[:, :, 2] * br_gqa
    return merged @ p['w_out']


def conv_ffn(h, p):
    u = depthwise_conv(h @ p['w_up'], p['ffn_dw_w'], p['ffn_dw_b'])
    a, g = jnp.split(u, 2, axis=-1)
    return (jax.nn.silu(g) * a) @ p['w_down']


def setup_inputs(seed: int = 0) -> dict:
    key = jax.random.key(seed)
    ks = jax.random.split(key, 30)
    L, D = DEPTH, D_MODEL

    def nrm(i, shape, scale):
        return jax.random.normal(ks[i], shape, jnp.float32) * scale

    def gain(i, shape):
        return 1.0 + 0.05 * jax.random.normal(ks[i], shape, jnp.float32)

    return {
        'x': nrm(0, (BATCH, SEQ, D), 1.0),
        'c': nrm(1, (BATCH, D), 1.0),
        'ctx': nrm(2, (BATCH, CTX_LEN, D), 1.0),
        'c_ctx': nrm(3, (D,), 1.0),
        'w_mod': nrm(4, (L, D, N_MOD * D), 0.5 * D ** -0.5),
        'b_mod': nrm(5, (L, N_MOD * D), 0.01),
        'g_norm1': gain(6, (L, D)),
        'g_norm2': gain(7, (L, D)),
        'w_in': nrm(8, (L, D, IN_COLS), D ** -0.5),
        'b_gate': nrm(9, (L, N_BRANCH * D), 0.01),
        'conv_dw_w': nrm(10, (L, CONV_WIDTH, CONV_DIM), CONV_WIDTH ** -0.5),
        'conv_dw_b': nrm(11, (L, CONV_DIM), 0.01),
        'conv_ln_g': gain(12, (L, CONV_DIM)),
        'conv_ln_b': nrm(13, (L, CONV_DIM), 0.01),
        'w_conv_out': nrm(14, (L, CONV_DIM, D), CONV_DIM ** -0.5),
        'g_q_a': gain(15, (L, MLA_Q_RANK)),
        'w_q_b': nrm(16, (L, MLA_Q_RANK, MLA_HEADS * MLA_QK), MLA_Q_RANK ** -0.5),
        'g_kv_a': gain(17, (L, MLA_KV_RANK)),
        'w_kv_b': nrm(18, (L, MLA_KV_RANK, MLA_HEADS * (MLA_NOPE + MLA_V)), MLA_KV_RANK ** -0.5),
        'g_mla_q': gain(19, (L, MLA_QK)),
        'g_mla_k': gain(20, (L, MLA_QK)),
        'w_mla_o': nrm(21, (L, MLA_HEADS * MLA_V, D), (MLA_HEADS * MLA_V) ** -0.5),
        'g_gqa_q': gain(22, (L, GQA_HEAD_DIM)),
        'g_gqa_k': gain(23, (L, GQA_HEAD_DIM)),
        'w_gqa_o': nrm(24, (L, GQA_HEADS * GQA_HEAD_DIM, D), (GQA_HEADS * GQA_HEAD_DIM) ** -0.5),
        'w_out': nrm(25, (L, D, D), D ** -0.5),
        'w_up': nrm(26, (L, D, 2 * D_FF), D ** -0.5),
        'ffn_dw_w': nrm(27, (L, FFN_CONV_WIDTH, 2 * D_FF), FFN_CONV_WIDTH ** -0.5),
        'ffn_dw_b': nrm(28, (L, 2 * D_FF), 0.01),
        'w_down': nrm(29, (L, D_FF, D), D_FF ** -0.5),
    }


def reference(x, c, ctx, c_ctx, w_mod, b_mod, g_norm1, g_norm2, w_in, b_gate, conv_dw_w, conv_dw_b,
              conv_ln_g, conv_ln_b, w_conv_out, g_q_a, w_q_b, g_kv_a, w_kv_b, g_mla_q, g_mla_k, w_mla_o,
              g_gqa_q, g_gqa_k, w_gqa_o, w_out, w_up, ffn_dw_w, ffn_dw_b, w_down):
    n = x.shape[1]
    n_rows = n // GRID_W
    row_idx = jnp.repeat(jnp.arange(n_rows, dtype=jnp.int32), GRID_W)
    col_idx = jnp.tile(jnp.arange(GRID_W, dtype=jnp.int32), n_rows)
    rope_m = axial_rope_tables(row_idx, col_idx, MLA_ROPE)
    rope_g = axial_rope_tables(row_idx, col_idx, GQA_HEAD_DIM)
    xc = ctx
    for layer in range(DEPTH):
        p = dict(w_in=w_in[layer], b_gate=b_gate[layer], conv_dw_w=conv_dw_w[layer], conv_dw_b=conv_dw_b[layer],
                 conv_ln_g=conv_ln_g[layer], conv_ln_b=conv_ln_b[layer], w_conv_out=w_conv_out[layer],
                 g_q_a=g_q_a[layer], w_q_b=w_q_b[layer], g_kv_a=g_kv_a[layer], w_kv_b=w_kv_b[layer],
                 g_mla_q=g_mla_q[layer], g_mla_k=g_mla_k[layer], w_mla_o=w_mla_o[layer],
                 g_gqa_q=g_gqa_q[layer], g_gqa_k=g_gqa_k[layer], w_gqa_o=w_gqa_o[layer], w_out=w_out[layer],
                 w_up=w_up[layer], ffn_dw_w=ffn_dw_w[layer], ffn_dw_b=ffn_dw_b[layer], w_down=w_down[layer])
        sh1, sc1, g1, sh2, sc2, g2 = adaln(c, w_mod[layer], b_mod[layer])
        csh1, csc1, cg1, csh2, csc2, cg2 = adaln(c_ctx[None, :], w_mod[layer], b_mod[layer])

        h = modulate(rms_norm(x, g_norm1[layer]), sh1, sc1)
        hc = modulate(rms_norm(xc, g_norm1[layer]), csh1, csc1)
        glu, gate, (qm, km, vm), (qg, kg, vg) = project_tokens(h, p, rope_m, rope_g)
        glu_c, gate_c, (qmc, kmc, vmc), (qgc, kgc, vgc) = project_tokens(hc, p, None, None)
        o_m = blocked_attention(qm, jnp.concatenate([kmc, km], axis=1), jnp.concatenate([vmc, vm], axis=1), MLA_SCALE)
        o_g = blocked_attention(qg, jnp.concatenate([kgc, kg], axis=1), jnp.concatenate([vgc, vg], axis=1), GQA_SCALE)
        x = x + g1 * merge_branches(glu, gate, o_m, o_g, p)
        x = x + g2 * conv_ffn(modulate(rms_norm(x, g_norm2[layer]), sh2, sc2), p)

        if layer < DEPTH - 1:
            oc_m = attention_block(qmc, kmc, vmc, MLA_SCALE)
            oc_g = attention_block(qgc, kgc, vgc, GQA_SCALE)
            xc = xc + cg1 * merge_branches(glu_c, gate_c, oc_m, oc_g, p)
            xc = xc + cg2 * conv_ffn(modulate(rms_norm(xc, g_norm2[layer]), csh2, csc2), p)
    return x
```

```python
from functools import partial

import jax
import jax.numpy as jnp
from jax import lax
from jax.experimental import pallas as pl
from jax.experimental.pallas import tpu as pltpu

D = 1024
SEQ = 16384
DEPTH = 4
GRID_W = 64
CTX = 256
NA = CTX + SEQ
ROPE_THETA = 10000.0
EPS = 1e-6
N_MOD = 6

CONV_DIM = 512
CONV_WIDTH = 31
MLA_HEADS = 8
MLA_Q_RANK = 384
MLA_KV_RANK = 256
MLA_NOPE = 64
MLA_ROPE = 32
MLA_V = 64
MLA_QK = MLA_NOPE + MLA_ROPE
GQA_HEADS = 8
GQA_KV_HEADS = 2
GQA_HEAD_DIM = 64
GQA_GROUP = GQA_HEADS // GQA_KV_HEADS
D_FF = 2816
MLA_SCALE = MLA_QK ** -0.5
GQA_SCALE = GQA_HEAD_DIM ** -0.5

HEAD_PAD = 128
TM = 256
NT = NA // TM
KC = 256
MQ = 256
HALO_Y = 16
HALO_X = 8
VMEM_LIMIT = 56 * 1024 * 1024

BF = jnp.bfloat16
F32 = jnp.float32

C_GLU = 0
C_GATE = 2 * CONV_DIM
C_QA = C_GATE + 3 * D
C_KVA = C_QA + MLA_Q_RANK
W_MAIN = C_KVA + MLA_KV_RANK
R_GQ = 0
R_GK = GQA_HEADS * GQA_HEAD_DIM
R_GV = R_GK + GQA_KV_HEADS * GQA_HEAD_DIM
R_KR = R_GV + GQA_KV_HEADS * GQA_HEAD_DIM
W_FM = R_KR + MLA_ROPE


def _const_spec(shape):
    nd = len(shape)
    return pl.BlockSpec(shape, lambda *_: (0,) * nd, pipeline_mode=pl.Buffered(1))


def _dot(a, b):
    return jnp.dot(a, b, preferred_element_type=F32)


def _dot_nt(a, b):
    return lax.dot_general(a, b, (((1,), (1,)), ((), ())), preferred_element_type=F32)


def _sigmoid(x):
    return 1.0 / (1.0 + jnp.exp(-x))


def _rope_rows(x, cos, sin):
    q = x.shape[0] // 4
    x1, x2, x3, x4 = x[0:q], x[q:2 * q], x[2 * q:3 * q], x[3 * q:4 * q]
    rot = jnp.concatenate([-x2, x1, -x4, x3], axis=0)
    return x * cos + rot * sin


MOD_TN = 1536


def _mod_kernel(cond_ref, w_ref, b_ref, o_ref):
    c = cond_ref[...]
    s = c * _sigmoid(c)
    w = w_ref[0]
    b = b_ref[0]
    for j in range(2):
        o_ref[0, j:j + 1, :] = jnp.sum(s[:, j:j + 1] * w, axis=0, keepdims=True) + b


def _mod_call(cond_t, w_mod, b_mod):
    nl = w_mod.shape[0]
    ncol = N_MOD * D
    return pl.pallas_call(
        _mod_kernel,
        grid=(nl, ncol // MOD_TN),
        in_specs=[
            pl.BlockSpec((D, 2), lambda l, j: (0, 0)),
            pl.BlockSpec((1, D, MOD_TN), lambda l, j: (l, 0, j)),
            pl.BlockSpec((1, 1, MOD_TN), lambda l, j: (l, 0, j)),
        ],
        out_specs=pl.BlockSpec((1, 2, MOD_TN), lambda l, j: (l, 0, j)),
        out_shape=jax.ShapeDtypeStruct((nl, 2, ncol), F32),
        compiler_params=pltpu.CompilerParams(vmem_limit_bytes=VMEM_LIMIT),
        name="adaln_mod",
    )(cond_t, w_mod, b_mod.reshape(nl, 1, ncol))


def _proj_kernel(x_ref, mod_ref, gn_ref, wmain_ref, wfm_ref, wqb_ref, wkvb_ref, bgate_ref,
                 gqa_ref, gkva_ref, gmq_ref, gmk_ref, ggq_ref, ggk_ref,
                 cosm_ref, sinm_ref, cosg_ref, sing_ref,
                 y_ref, gates_ref, qtm_ref, km_ref, vtm_ref, qtg_ref, kg_ref, vtg_ref):
    x = x_ref[...]
    shift = mod_ref[0, 0:1, :]
    scale = mod_ref[0, 1:2, :]
    h = x * lax.rsqrt(jnp.mean(x * x, axis=-1, keepdims=True) + EPS) * gn_ref[...]
    hb = (h * (1.0 + scale) + shift).astype(BF)

    za = _dot(hb, wmain_ref[:, C_GLU:C_GLU + CONV_DIM])
    zg = _dot(hb, wmain_ref[:, C_GLU + CONV_DIM:C_GATE])
    y_ref[...] = za * _sigmoid(zg)
    zgate = _dot(hb, wmain_ref[:, C_GATE:C_QA])
    gates_ref[...] = _sigmoid(zgate + bgate_ref[...])

    zqa = _dot(hb, wmain_ref[:, C_QA:C_KVA])
    qa = (zqa * lax.rsqrt(jnp.mean(zqa * zqa, axis=-1, keepdims=True) + EPS) * gqa_ref[...]).astype(BF)
    zkva = _dot(hb, wmain_ref[:, C_KVA:W_MAIN])
    kva = (zkva * lax.rsqrt(jnp.mean(zkva * zkva, axis=-1, keepdims=True) + EPS) * gkva_ref[...]).astype(BF)
    qt = _dot_nt(wqb_ref[...], qa)
    kvt = _dot_nt(wkvb_ref[...], kva)
    fm = _dot_nt(wfm_ref[...], hb)

    cosm, sinm = cosm_ref[...], sinm_ref[...]
    cosg, sing = cosg_ref[...], sing_ref[...]
    gmq, gmk = gmq_ref[...], gmk_ref[...]
    ggq, ggk = ggq_ref[...], ggk_ref[...]
    kr = fm[R_KR:R_KR + MLA_ROPE]
    kr_ss = jnp.sum(kr * kr, axis=0, keepdims=True)
    zpad = jnp.zeros((HEAD_PAD - MLA_QK, TM), F32)

    for hd in range(MLA_HEADS):
        q = qt[hd * HEAD_PAD:(hd + 1) * HEAD_PAD]
        r = lax.rsqrt(jnp.sum(q * q, axis=0, keepdims=True) * (1.0 / MLA_QK) + EPS)
        qn = q * r * gmq
        qr = _rope_rows(qn[MLA_NOPE:MLA_QK], cosm, sinm)
        qfull = jnp.concatenate([qn[:MLA_NOPE], qr, zpad], axis=0) * MLA_SCALE
        qtm_ref[hd] = qfull.astype(BF)

        base = hd * (MLA_NOPE + MLA_V)
        kn = kvt[base:base + MLA_NOPE]
        vtm_ref[0, hd * MLA_V:(hd + 1) * MLA_V, :] = kvt[base + MLA_NOPE:base + MLA_NOPE + MLA_V].astype(BF)
        rk = lax.rsqrt((jnp.sum(kn * kn, axis=0, keepdims=True) + kr_ss) * (1.0 / MLA_QK) + EPS)
        knn = kn * rk * gmk[:MLA_NOPE]
        krn = _rope_rows(kr * rk * gmk[MLA_NOPE:MLA_QK], cosm, sinm)
        kfull = jnp.concatenate([knn, krn, zpad], axis=0)
        km_ref[hd] = kfull.T.astype(BF)

    zhalf = jnp.zeros((GQA_HEAD_DIM, TM), F32)
    for hd in range(GQA_HEADS):
        q = fm[R_GQ + hd * GQA_HEAD_DIM:R_GQ + (hd + 1) * GQA_HEAD_DIM]
        r = lax.rsqrt(jnp.mean(q * q, axis=0, keepdims=True) + EPS)
        qn = _rope_rows(q * r * ggq, cosg, sing) * GQA_SCALE
        parts = [qn, zhalf] if hd // GQA_GROUP == 0 else [zhalf, qn]
        qtg_ref[hd] = jnp.concatenate(parts, axis=0).astype(BF)
    ks = []
    for g in range(GQA_KV_HEADS):
        k = fm[R_GK + g * GQA_HEAD_DIM:R_GK + (g + 1) * GQA_HEAD_DIM]
        r = lax.rsqrt(jnp.mean(k * k, axis=0, keepdims=True) + EPS)
        ks.append(_rope_rows(k * r * ggk, cosg, sing))
    kg_ref[...] = jnp.concatenate(ks, axis=0).T.astype(BF)
    vtg_ref[0] = fm[R_GV:R_KR].astype(BF)


def _proj_call(xall, mod_l, gn, wmain, wfm, wqb, wkvb, bgate, gqa, gkva, gmq, gmk, ggq, ggk,
               cosm, sinm, cosg, sing):
    def stream(i):
        return jnp.where(i == 0, 1, 0)

    in_specs = [
        pl.BlockSpec((TM, D), lambda i: (i, 0)),
        pl.BlockSpec((1, N_MOD, D), lambda i: (stream(i), 0, 0)),
        _const_spec((1, D)),
        _const_spec((D, W_MAIN)),
        _const_spec((W_FM, D)),
        _const_spec((MLA_HEADS * HEAD_PAD, MLA_Q_RANK)),
        _const_spec((MLA_HEADS * (MLA_NOPE + MLA_V), MLA_KV_RANK)),
        _const_spec((1, 3 * D)),
        _const_spec((1, MLA_Q_RANK)),
        _const_spec((1, MLA_KV_RANK)),
        _const_spec((HEAD_PAD, 1)),
        _const_spec((HEAD_PAD, 1)),
        _const_spec((GQA_HEAD_DIM, 1)),
        _const_spec((GQA_HEAD_DIM, 1)),
        pl.BlockSpec((MLA_ROPE, TM), lambda i: (0, i)),
        pl.BlockSpec((MLA_ROPE, TM), lambda i: (0, i)),
        pl.BlockSpec((GQA_HEAD_DIM, TM), lambda i: (0, i)),
        pl.BlockSpec((GQA_HEAD_DIM, TM), lambda i: (0, i)),
    ]
    out_specs = [
        pl.BlockSpec((TM, CONV_DIM), lambda i: (i, 0)),
        pl.BlockSpec((TM, 3 * D), lambda i: (i, 0)),
        pl.BlockSpec((MLA_HEADS, HEAD_PAD, TM), lambda i: (0, 0, i)),
        pl.BlockSpec((MLA_HEADS, TM, HEAD_PAD), lambda i: (0, i, 0)),
        pl.BlockSpec((1, MLA_HEADS * MLA_V, KC), lambda i: (i, 0, 0)),
        pl.BlockSpec((GQA_HEADS, HEAD_PAD, TM), lambda i: (0, 0, i)),
        pl.BlockSpec((TM, HEAD_PAD), lambda i: (i, 0)),
        pl.BlockSpec((1, GQA_KV_HEADS * GQA_HEAD_DIM, KC), lambda i: (i, 0, 0)),
    ]
    out_shape = [
        jax.ShapeDtypeStruct((NA, CONV_DIM), F32),
        jax.ShapeDtypeStruct((NA, 3 * D), F32),
        jax.ShapeDtypeStruct((MLA_HEADS, HEAD_PAD, NA), BF),
        jax.ShapeDtypeStruct((MLA_HEADS, NA, HEAD_PAD), BF),
        jax.ShapeDtypeStruct((NT, MLA_HEADS * MLA_V, KC), BF),
        jax.ShapeDtypeStruct((GQA_HEADS, HEAD_PAD, NA), BF),
        jax.ShapeDtypeStruct((NA, HEAD_PAD), BF),
        jax.ShapeDtypeStruct((NT, GQA_KV_HEADS * GQA_HEAD_DIM, KC), BF),
    ]
    return pl.pallas_call(
        _proj_kernel,
        grid=(NT,),
        in_specs=in_specs,
        out_specs=out_specs,
        out_shape=out_shape,
        compiler_params=pltpu.CompilerParams(dimension_semantics=("arbitrary",),
                                             vmem_limit_bytes=VMEM_LIMIT),
        name="proj",
    )(xall, mod_l, gn, wmain, wfm, wqb, wkvb, bgate, gqa, gkva, gmq, gmk, ggq, ggk,
      cosm, sinm, cosg, sing)


def _attn_kernel(q_ref, k_ref, vt_ref, o_ref, *, k_per_head, v_per_head):
    j = pl.program_id(1)
    n_chunks = jnp.where(j == 0, 1, NT)
    outs = []
    for hh in range(2):
        qt = q_ref[hh]
        kh = hh if k_per_head else 0
        v0 = hh * MLA_V if v_per_head else 0

        def scores(c):
            start = pl.multiple_of(c * KC, KC)
            return _dot(k_ref[kh, pl.ds(start, KC), :], qt)

        def pv(c, p):
            return _dot(vt_ref[c, v0:v0 + MLA_V, :], p.astype(BF))

        s = scores(0)
        m = jnp.max(s, axis=0, keepdims=True)
        p = jnp.exp(s - m)
        l = jnp.sum(p, axis=0, keepdims=True)
        acc = pv(0, p)

        def body(c, carry):
            m, l, acc = carry
            s = scores(c)
            m_new = jnp.maximum(m, jnp.max(s, axis=0, keepdims=True))
            alpha = jnp.exp(m - m_new)
            p = jnp.exp(s - m_new)
            l = alpha * l + jnp.sum(p, axis=0, keepdims=True)
            acc = alpha * acc + pv(c, p)
            return m_new, l, acc

        m, l, acc = lax.fori_loop(1, n_chunks, body, (m, l, acc))
        outs.append(acc / l)
    o_ref[...] = jnp.concatenate(outs, axis=0).T.astype(o_ref.dtype)


def _attn_call(qt, k, vt, *, k_per_head, v_per_head, name):
    if k_per_head:
        k_spec = pl.BlockSpec((2, NA, HEAD_PAD), lambda p, j: (p, 0, 0))
    else:
        k_spec = pl.BlockSpec((1, NA, HEAD_PAD), lambda p, j: (0, 0, 0))
    if v_per_head:
        v_spec = pl.BlockSpec((NT, 2 * MLA_V, KC), lambda p, j: (0, p, 0))
    else:
        v_spec = pl.BlockSpec((NT, GQA_HEAD_DIM, KC), lambda p, j: (0, p // (GQA_GROUP // 2), 0))
    return pl.pallas_call(
        partial(_attn_kernel, k_per_head=k_per_head, v_per_head=v_per_head),
        grid=(4, NA // MQ),
        in_specs=[pl.BlockSpec((2, HEAD_PAD, MQ), lambda p, j: (p, 0, j)), k_spec, v_spec],
        out_specs=pl.BlockSpec((MQ, 2 * MLA_V), lambda p, j: (j, p)),
        out_shape=jax.ShapeDtypeStruct((NA, 8 * MLA_V), BF),
        compiler_params=pltpu.CompilerParams(dimension_semantics=("arbitrary", "arbitrary"),
                                             vmem_limit_bytes=VMEM_LIMIT),
        name=name,
    )(qt, k, vt)


def _merge_kernel(x_ref, yp_ref, yc_ref, yn_ref, gates_ref, om_ref, og_ref, mod_ref,
                  cw_ref, cb_ref, lng_ref, lnb_ref, wco_ref, wmo_ref, wgo_ref, wout_ref,
                  o_ref, buf_ref):
    i = pl.program_id(0)
    left_ok = i >= 2
    right_ok = jnp.logical_and(i >= 1, i < NT - 1)
    buf_ref[0:HALO_Y, :] = jnp.where(left_ok, yp_ref[...], 0.0)
    buf_ref[HALO_Y:HALO_Y + TM, :] = yc_ref[...]
    buf_ref[HALO_Y + TM:HALO_Y + TM + HALO_Y, :] = jnp.where(right_ok, yn_ref[...], 0.0)

    off = HALO_Y - CONV_WIDTH // 2
    acc = jnp.zeros((TM, CONV_DIM), F32) + cb_ref[...]
    for k in range(CONV_WIDTH):
        acc = acc + buf_ref[off + k:off + k + TM, :] * cw_ref[k:k + 1, :]
    mu = jnp.mean(acc, axis=-1, keepdims=True)
    xc = acc - mu
    var = jnp.mean(xc * xc, axis=-1, keepdims=True)
    ln = xc * lax.rsqrt(var + EPS) * lng_ref[...] + lnb_ref[...]
    act = (ln * _sigmoid(ln)).astype(BF)
    br_conv = _dot(act, wco_ref[...])
    br_mla = _dot(om_ref[...], wmo_ref[...])
    br_gqa = _dot(og_ref[...], wgo_ref[...])
    merged = (gates_ref[:, 0:D] * br_conv + gates_ref[:, D:2 * D] * br_mla
              + gates_ref[:, 2 * D:3 * D] * br_gqa)
    res = _dot(merged.astype(BF), wout_ref[...])
    o_ref[...] = x_ref[...] + mod_ref[0, 2:3, :] * res


def _merge_call(xall, y, gates, om, og, mod_l, cw, cb, lng, lnb, wco, wmo, wgo, wout):
    ny = NA // HALO_Y
    per = TM // HALO_Y
    in_specs = [
        pl.BlockSpec((TM, D), lambda i: (i, 0)),
        pl.BlockSpec((HALO_Y, CONV_DIM), lambda i: (jnp.maximum(i * per - 1, 0), 0)),
        pl.BlockSpec((TM, CONV_DIM), lambda i: (i, 0)),
        pl.BlockSpec((HALO_Y, CONV_DIM), lambda i: (jnp.minimum((i + 1) * per, ny - 1), 0)),
        pl.BlockSpec((TM, 3 * D), lambda i: (i, 0)),
        pl.BlockSpec((TM, 8 * MLA_V), lambda i: (i, 0)),
        pl.BlockSpec((TM, 8 * MLA_V), lambda i: (i, 0)),
        pl.BlockSpec((1, N_MOD, D), lambda i: (jnp.where(i == 0, 1, 0), 0, 0)),
        _const_spec((CONV_WIDTH, CONV_DIM)),
        _const_spec((1, CONV_DIM)),
        _const_spec((1, CONV_DIM)),
        _const_spec((1, CONV_DIM)),
        _const_spec((CONV_DIM, D)),
        _const_spec((8 * MLA_V, D)),
        _const_spec((8 * GQA_HEAD_DIM, D)),
        _const_spec((D, D)),
    ]
    return pl.pallas_call(
        _merge_kernel,
        grid=(NT,),
        in_specs=in_specs,
        out_specs=pl.BlockSpec((TM, D), lambda i: (i, 0)),
        out_shape=jax.ShapeDtypeStruct((NA, D), F32),
        scratch_shapes=[pltpu.VMEM((TM + 2 * HALO_Y, CONV_DIM), F32)],
        compiler_params=pltpu.CompilerParams(dimension_semantics=("arbitrary",),
                                             vmem_limit_bytes=VMEM_LIMIT),
        name="merge",
    )(xall, y, y, y, gates, om, og, mod_l, cw, cb, lng, lnb, wco, wmo, wgo, wout)


FF_CHUNK = D_FF // 2


def _ffn_kernel(xp_ref, xc_ref, xn_ref, mod_ref, gn_ref, wup_ref, dw_ref, db_ref, wdown_ref,
                o_ref, ua_ref, ug_ref):
    i = pl.program_id(0)
    left_ok = i >= 2
    right_ok = jnp.logical_and(i >= 1, i < NT - 1)
    xc = xc_ref[...]
    xh = jnp.concatenate([xp_ref[...], xc, xn_ref[...]], axis=0)
    shift = mod_ref[0, 3:4, :]
    scale = mod_ref[0, 4:5, :]
    h = xh * lax.rsqrt(jnp.mean(xh * xh, axis=-1, keepdims=True) + EPS) * gn_ref[...]
    hb = (h * (1.0 + scale) + shift).astype(BF)

    rows = TM + 2 * HALO_X
    row_id = lax.broadcasted_iota(jnp.int32, (rows, 1), 0)
    keep = jnp.logical_and(jnp.logical_or(row_id >= HALO_X, left_ok),
                           jnp.logical_or(row_id < HALO_X + TM, right_ok))

    def conv3(u_ref, col0, width):
        w = dw_ref[:, col0:col0 + width]
        return (u_ref[HALO_X - 1:HALO_X - 1 + TM, :] * w[0:1]
                + u_ref[HALO_X:HALO_X + TM, :] * w[1:2]
                + u_ref[HALO_X + 1:HALO_X + 1 + TM, :] * w[2:3]
                + db_ref[:, col0:col0 + width])

    acc = jnp.zeros((TM, D), F32)
    for c in range(D_FF // FF_CHUNK):
        a0 = c * FF_CHUNK
        g0 = D_FF + c * FF_CHUNK
        ua_ref[...] = jnp.where(keep, _dot(hb, wup_ref[:, a0:a0 + FF_CHUNK]), 0.0)
        ug_ref[...] = jnp.where(keep, _dot(hb, wup_ref[:, g0:g0 + FF_CHUNK]), 0.0)
        a = conv3(ua_ref, a0, FF_CHUNK)
        g = conv3(ug_ref, g0, FF_CHUNK)
        act = (g * _sigmoid(g) * a).astype(BF)
        acc = acc + _dot(act, wdown_ref[a0:a0 + FF_CHUNK, :])
    o_ref[...] = xc + mod_ref[0, 5:6, :] * acc


def _ffn_call(x1, mod_l, gn, wup, dw, db, wdown):
    nx = NA // HALO_X
    per = TM // HALO_X
    in_specs = [
        pl.BlockSpec((HALO_X, D), lambda i: (jnp.maximum(i * per - 1, 0), 0)),
        pl.BlockSpec((TM, D), lambda i: (i, 0)),
        pl.BlockSpec((HALO_X, D), lambda i: (jnp.minimum((i + 1) * per, nx - 1), 0)),
        pl.BlockSpec((1, N_MOD, D), lambda i: (jnp.where(i == 0, 1, 0), 0, 0)),
        _const_spec((1, D)),
        _const_spec((D, 2 * D_FF)),
        _const_spec((3, 2 * D_FF)),
        _const_spec((1, 2 * D_FF)),
        _const_spec((D_FF, D)),
    ]
    return pl.pallas_call(
        _ffn_kernel,
        grid=(NT,),
        in_specs=in_specs,
        out_specs=pl.BlockSpec((TM, D), lambda i: (i, 0)),
        out_shape=jax.ShapeDtypeStruct((NA, D), F32),
        scratch_shapes=[pltpu.VMEM((TM + 2 * HALO_X, FF_CHUNK), F32),
                        pltpu.VMEM((TM + 2 * HALO_X, FF_CHUNK), F32)],
        compiler_params=pltpu.CompilerParams(dimension_semantics=("arbitrary",),
                                             vmem_limit_bytes=VMEM_LIMIT),
        name="ffn",
    )(x1, x1, x1, mod_l, gn, wup, dw, db, wdown)


def _rope_tables(rot_dim):
    n_freq = rot_dim // 4
    inv = 1.0 / (ROPE_THETA ** (jnp.arange(n_freq, dtype=F32) / n_freq))
    pos = jnp.arange(SEQ, dtype=jnp.int32)
    ang_r = (pos // GRID_W).astype(F32)[None, :] * inv[:, None]
    ang_c = (pos % GRID_W).astype(F32)[None, :] * inv[:, None]
    ang = jnp.concatenate([ang_r, ang_r, ang_c, ang_c], axis=0)
    cos = jnp.concatenate([jnp.ones((rot_dim, CTX), F32), jnp.cos(ang)], axis=1)
    sin = jnp.concatenate([jnp.zeros((rot_dim, CTX), F32), jnp.sin(ang)], axis=1)
    return cos, sin


def _pad_col(g, n):
    return jnp.pad(g, (0, n - g.shape[0])).reshape(n, 1)


def kernel(x, c, ctx, c_ctx, w_mod, b_mod, g_norm1, g_norm2, w_in, b_gate, conv_dw_w, conv_dw_b, conv_ln_g, conv_ln_b, w_conv_out, g_q_a, w_q_b, g_kv_a, w_kv_b, g_mla_q, g_mla_k, w_mla_o, g_gqa_q, g_gqa_k, w_gqa_o, w_out, w_up, ffn_dw_w, ffn_dw_b, w_down):
    assert x.shape == (1, SEQ, D) and ctx.shape == (1, CTX, D)
    xall = jnp.concatenate([ctx[0], x[0]], axis=0)
    cond_t = jnp.stack([c[0], c_ctx], axis=1)
    mod = _mod_call(cond_t, w_mod, b_mod).reshape(DEPTH, 2, N_MOD, D)
    cosm, sinm = _rope_tables(MLA_ROPE)
    cosg, sing = _rope_tables(GQA_HEAD_DIM)

    o_qa = 2 * CONV_DIM
    o_kva = o_qa + MLA_Q_RANK
    o_kr = o_kva + MLA_KV_RANK
    o_gq = o_kr + MLA_ROPE
    o_gk = o_gq + GQA_HEADS * GQA_HEAD_DIM
    o_gv = o_gk + GQA_KV_HEADS * GQA_HEAD_DIM
    o_gate = o_gv + GQA_KV_HEADS * GQA_HEAD_DIM

    for l in range(DEPTH):
        wi = w_in[l]
        wmain = jnp.concatenate([wi[:, :o_qa], wi[:, o_gate:], wi[:, o_qa:o_kr]], axis=1).astype(BF)
        wfm = jnp.concatenate([wi[:, o_gq:o_gate], wi[:, o_kr:o_gq]], axis=1).T.astype(BF)
        wqb = jnp.pad(w_q_b[l].reshape(MLA_Q_RANK, MLA_HEADS, MLA_QK),
                      ((0, 0), (0, 0), (0, HEAD_PAD - MLA_QK)))
        wqb = wqb.reshape(MLA_Q_RANK, MLA_HEADS * HEAD_PAD).T.astype(BF)
        wkvb = w_kv_b[l].T.astype(BF)
        mod_l = mod[l]

        y, gates, qtm, km, vtm, qtg, kg, vtg = _proj_call(
            xall, mod_l, g_norm1[l].reshape(1, D), wmain, wfm, wqb, wkvb, b_gate[l].reshape(1, 3 * D),
            g_q_a[l].reshape(1, -1), g_kv_a[l].reshape(1, -1),
            _pad_col(g_mla_q[l], HEAD_PAD), _pad_col(g_mla_k[l], HEAD_PAD),
            g_gqa_q[l].reshape(-1, 1), g_gqa_k[l].reshape(-1, 1), cosm, sinm, cosg, sing)
        om = _attn_call(qtm, km, vtm, k_per_head=True, v_per_head=True, name="attn_mla")
        og = _attn_call(qtg, kg.reshape(1, NA, HEAD_PAD), vtg, k_per_head=False, v_per_head=False,
                        name="attn_gqa")
        x1 = _merge_call(xall, y, gates, om, og, mod_l, conv_dw_w[l], conv_dw_b[l].reshape(1, -1),
                         conv_ln_g[l].reshape(1, -1), conv_ln_b[l].reshape(1, -1),
                         w_conv_out[l].astype(BF), w_mla_o[l].astype(BF), w_gqa_o[l].astype(BF),
                         w_out[l].astype(BF))
        xall = _ffn_call(x1, mod_l, g_norm2[l].reshape(1, D), w_up[l].astype(BF), ffn_dw_w[l],
                         ffn_dw_b[l].reshape(1, -1), w_down[l].astype(BF))
    return xall[CTX:][None]
```

```python
from functools import partial

import jax
import jax.numpy as jnp
from jax import lax
from jax.experimental import pallas as pl
from jax.experimental.pallas import tpu as pltpu

D = 1024
SEQ = 16384
DEPTH = 4
GRID_W = 64
CTX = 256
NA = CTX + SEQ
ROPE_THETA = 10000.0
EPS = 1e-6
N_MOD = 6

CONV_DIM = 512
CONV_WIDTH = 31
MLA_HEADS = 8
MLA_Q_RANK = 384
MLA_KV_RANK = 256
MLA_NOPE = 64
MLA_ROPE = 32
MLA_V = 64
MLA_QK = MLA_NOPE + MLA_ROPE
GQA_HEADS = 8
GQA_KV_HEADS = 2
GQA_HEAD_DIM = 64
GQA_GROUP = GQA_HEADS // GQA_KV_HEADS
D_FF = 2816
LOG2E = 1.4426950408889634
MLA_SCALE = MLA_QK ** -0.5 * LOG2E
GQA_SCALE = GQA_HEAD_DIM ** -0.5 * LOG2E

HEAD_PAD = 128
TM = 256
NT = NA // TM
KC = 256
KCH = 1280
NCH = NA // KCH
KSUB = KCH // KC
MQ = 256
HALO_Y = 16
HALO_X = 8
VMEM_LIMIT = 56 * 1024 * 1024

BF = jnp.bfloat16
F32 = jnp.float32

C_GLU = 0
C_GATE = 2 * CONV_DIM
C_QA = C_GATE + 3 * D
C_KVA = C_QA + MLA_Q_RANK
W_MAIN = C_KVA + MLA_KV_RANK
R_GQ = 0
R_GK = GQA_HEADS * GQA_HEAD_DIM
R_GV = R_GK + GQA_KV_HEADS * GQA_HEAD_DIM
R_KR = R_GV + GQA_KV_HEADS * GQA_HEAD_DIM
W_FM = R_KR + MLA_ROPE


def _const_spec(shape):
    nd = len(shape)
    return pl.BlockSpec(shape, lambda *_: (0,) * nd, pipeline_mode=pl.Buffered(1))


def _dot(a, b):
    return jnp.dot(a, b, preferred_element_type=F32)


def _dot_nt(a, b):
    return lax.dot_general(a, b, (((1,), (1,)), ((), ())), preferred_element_type=F32)


def _sigmoid(x):
    return 1.0 / (1.0 + jnp.exp(-x))


def _rope_rows(x, cos, sin):
    q = x.shape[0] // 4
    x1, x2, x3, x4 = x[0:q], x[q:2 * q], x[2 * q:3 * q], x[3 * q:4 * q]
    rot = jnp.concatenate([-x2, x1, -x4, x3], axis=0)
    return x * cos + rot * sin


MOD_TN = 1536


def _mod_kernel(cond_ref, w_ref, b_ref, o_ref):
    c = cond_ref[...]
    s = c * _sigmoid(c)
    w = w_ref[0]
    b = b_ref[0]
    for j in range(2):
        o_ref[0, j:j + 1, :] = jnp.sum(s[:, j:j + 1] * w, axis=0, keepdims=True) + b


def _mod_call(cond_t, w_mod, b_mod):
    nl = w_mod.shape[0]
    ncol = N_MOD * D
    return pl.pallas_call(
        _mod_kernel,
        grid=(nl, ncol // MOD_TN),
        in_specs=[
            pl.BlockSpec((D, 2), lambda l, j: (0, 0)),
            pl.BlockSpec((1, D, MOD_TN), lambda l, j: (l, 0, j)),
            pl.BlockSpec((1, 1, MOD_TN), lambda l, j: (l, 0, j)),
        ],
        out_specs=pl.BlockSpec((1, 2, MOD_TN), lambda l, j: (l, 0, j)),
        out_shape=jax.ShapeDtypeStruct((nl, 2, ncol), F32),
        compiler_params=pltpu.CompilerParams(vmem_limit_bytes=VMEM_LIMIT),
        name="adaln_mod",
    )(cond_t, w_mod, b_mod.reshape(nl, 1, ncol))


def _proj_kernel(x_ref, mod_ref, gn_ref, wmain_ref, wfm_ref, wqb_ref, wkvb_ref, bgate_ref,
                 gqa_ref, gkva_ref, gmq_ref, gmk_ref, ggq_ref, ggk_ref,
                 cosm_ref, sinm_ref, cosg_ref, sing_ref,
                 y_ref, gates_ref, qtm_ref, km_ref, vtm_ref, qtg_ref, kg_ref, vtg_ref):
    x = x_ref[...]
    shift = mod_ref[0, 0:1, :]
    scale = mod_ref[0, 1:2, :]
    h = x * lax.rsqrt(jnp.mean(x * x, axis=-1, keepdims=True) + EPS) * gn_ref[...]
    hb = (h * (1.0 + scale) + shift).astype(BF)

    za = _dot(hb, wmain_ref[:, C_GLU:C_GLU + CONV_DIM])
    zg = _dot(hb, wmain_ref[:, C_GLU + CONV_DIM:C_GATE])
    y_ref[...] = za * _sigmoid(zg)
    zgate = _dot(hb, wmain_ref[:, C_GATE:C_QA])
    gates_ref[...] = _sigmoid(zgate + bgate_ref[...])

    zqa = _dot(hb, wmain_ref[:, C_QA:C_KVA])
    qa = (zqa * lax.rsqrt(jnp.mean(zqa * zqa, axis=-1, keepdims=True) + EPS) * gqa_ref[...]).astype(BF)
    zkva = _dot(hb, wmain_ref[:, C_KVA:W_MAIN])
    kva = (zkva * lax.rsqrt(jnp.mean(zkva * zkva, axis=-1, keepdims=True) + EPS) * gkva_ref[...]).astype(BF)
    qt = _dot_nt(wqb_ref[...], qa)
    kvt = _dot_nt(wkvb_ref[...], kva)
    fm = _dot_nt(wfm_ref[...], hb)

    cosm, sinm = cosm_ref[...], sinm_ref[...]
    cosg, sing = cosg_ref[...], sing_ref[...]
    gmq, gmk = gmq_ref[...], gmk_ref[...]
    ggq, ggk = ggq_ref[...], ggk_ref[...]
    kr = fm[R_KR:R_KR + MLA_ROPE]
    kr_ss = jnp.sum(kr * kr, axis=0, keepdims=True)
    zpad = jnp.zeros((HEAD_PAD - MLA_QK, TM), F32)

    for hd in range(MLA_HEADS):
        q = qt[hd * HEAD_PAD:(hd + 1) * HEAD_PAD]
        r = lax.rsqrt(jnp.sum(q * q, axis=0, keepdims=True) * (1.0 / MLA_QK) + EPS)
        qn = q * r * gmq
        qr = _rope_rows(qn[MLA_NOPE:MLA_QK], cosm, sinm)
        qfull = jnp.concatenate([qn[:MLA_NOPE], qr, zpad], axis=0) * MLA_SCALE
        qtm_ref[hd] = qfull.astype(BF)

        base = hd * (MLA_NOPE + MLA_V)
        kn = kvt[base:base + MLA_NOPE]
        vtm_ref[0, hd * MLA_V:(hd + 1) * MLA_V, :] = kvt[base + MLA_NOPE:base + MLA_NOPE + MLA_V].astype(BF)
        rk = lax.rsqrt((jnp.sum(kn * kn, axis=0, keepdims=True) + kr_ss) * (1.0 / MLA_QK) + EPS)
        knn = kn * rk * gmk[:MLA_NOPE]
        krn = _rope_rows(kr * rk * gmk[MLA_NOPE:MLA_QK], cosm, sinm)
        kfull = jnp.concatenate([knn, krn, zpad], axis=0)
        km_ref[hd] = kfull.T.astype(BF)

    zhalf = jnp.zeros((GQA_HEAD_DIM, TM), F32)
    for hd in range(GQA_HEADS):
        q = fm[R_GQ + hd * GQA_HEAD_DIM:R_GQ + (hd + 1) * GQA_HEAD_DIM]
        r = lax.rsqrt(jnp.mean(q * q, axis=0, keepdims=True) + EPS)
        qn = _rope_rows(q * r * ggq, cosg, sing) * GQA_SCALE
        parts = [qn, zhalf] if hd // GQA_GROUP == 0 else [zhalf, qn]
        qtg_ref[hd] = jnp.concatenate(parts, axis=0).astype(BF)
    ks = []
    for g in range(GQA_KV_HEADS):
        k = fm[R_GK + g * GQA_HEAD_DIM:R_GK + (g + 1) * GQA_HEAD_DIM]
        r = lax.rsqrt(jnp.mean(k * k, axis=0, keepdims=True) + EPS)
        ks.append(_rope_rows(k * r * ggk, cosg, sing))
    kg_ref[...] = jnp.concatenate(ks, axis=0).T.astype(BF)
    vtg_ref[0] = fm[R_GV:R_KR].astype(BF)


def _proj_call(xall, mod_l, gn, wmain, wfm, wqb, wkvb, bgate, gqa, gkva, gmq, gmk, ggq, ggk,
               cosm, sinm, cosg, sing):
    def stream(i):
        return jnp.where(i == 0, 1, 0)

    in_specs = [
        pl.BlockSpec((TM, D), lambda i: (i, 0)),
        pl.BlockSpec((1, N_MOD, D), lambda i: (stream(i), 0, 0)),
        _const_spec((1, D)),
        _const_spec((D, W_MAIN)),
        _const_spec((W_FM, D)),
        _const_spec((MLA_HEADS * HEAD_PAD, MLA_Q_RANK)),
        _const_spec((MLA_HEADS * (MLA_NOPE + MLA_V), MLA_KV_RANK)),
        _const_spec((1, 3 * D)),
        _const_spec((1, MLA_Q_RANK)),
        _const_spec((1, MLA_KV_RANK)),
        _const_spec((HEAD_PAD, 1)),
        _const_spec((HEAD_PAD, 1)),
        _const_spec((GQA_HEAD_DIM, 1)),
        _const_spec((GQA_HEAD_DIM, 1)),
        pl.BlockSpec((MLA_ROPE, TM), lambda i: (0, i)),
        pl.BlockSpec((MLA_ROPE, TM), lambda i: (0, i)),
        pl.BlockSpec((GQA_HEAD_DIM, TM), lambda i: (0, i)),
        pl.BlockSpec((GQA_HEAD_DIM, TM), lambda i: (0, i)),
    ]
    out_specs = [
        pl.BlockSpec((TM, CONV_DIM), lambda i: (i, 0)),
        pl.BlockSpec((TM, 3 * D), lambda i: (i, 0)),
        pl.BlockSpec((MLA_HEADS, HEAD_PAD, TM), lambda i: (0, 0, i)),
        pl.BlockSpec((MLA_HEADS, TM, HEAD_PAD), lambda i: (0, i, 0)),
        pl.BlockSpec((1, MLA_HEADS * MLA_V, KC), lambda i: (i // KSUB, 0, i % KSUB)),
        pl.BlockSpec((GQA_HEADS, HEAD_PAD, TM), lambda i: (0, 0, i)),
        pl.BlockSpec((TM, HEAD_PAD), lambda i: (i, 0)),
        pl.BlockSpec((1, GQA_KV_HEADS * GQA_HEAD_DIM, KC), lambda i: (i // KSUB, 0, i % KSUB)),
    ]
    out_shape = [
        jax.ShapeDtypeStruct((NA, CONV_DIM), F32),
        jax.ShapeDtypeStruct((NA, 3 * D), F32),
        jax.ShapeDtypeStruct((MLA_HEADS, HEAD_PAD, NA), BF),
        jax.ShapeDtypeStruct((MLA_HEADS, NA, HEAD_PAD), BF),
        jax.ShapeDtypeStruct((NCH, MLA_HEADS * MLA_V, KCH), BF),
        jax.ShapeDtypeStruct((GQA_HEADS, HEAD_PAD, NA), BF),
        jax.ShapeDtypeStruct((NA, HEAD_PAD), BF),
        jax.ShapeDtypeStruct((NCH, GQA_KV_HEADS * GQA_HEAD_DIM, KCH), BF),
    ]
    return pl.pallas_call(
        _proj_kernel,
        grid=(NT,),
        in_specs=in_specs,
        out_specs=out_specs,
        out_shape=out_shape,
        compiler_params=pltpu.CompilerParams(dimension_semantics=("arbitrary",),
                                             vmem_limit_bytes=VMEM_LIMIT),
        name="proj",
    )(xall, mod_l, gn, wmain, wfm, wqb, wkvb, bgate, gqa, gkva, gmq, gmk, ggq, ggk,
      cosm, sinm, cosg, sing)


def _attn_kernel(q_ref, k_ref, vt_ref, o_ref, s_buf, p_buf, acc_ref, *, k_per_head, v_per_head):
    j = pl.program_id(1)

    def k_head(hh):
        return hh if k_per_head else 0

    def v_rows(hh):
        v0 = hh * MLA_V if v_per_head else 0
        return slice(v0, v0 + MLA_V)

    @pl.when(j == 0)
    def _():
        outs = []
        for hh in range(2):
            s = _dot(k_ref[k_head(hh), 0:CTX, :], q_ref[hh])
            p = jnp.exp2(s - jnp.max(s, axis=0, keepdims=True))
            l = jnp.sum(p, axis=0, keepdims=True)
            outs.append(_dot(vt_ref[0, v_rows(hh), 0:CTX], p.astype(BF)) / l)
        o_ref[...] = jnp.concatenate(outs, axis=0).T.astype(o_ref.dtype)

    @pl.when(j > 0)
    def _():
        def qk(hh, c, slot):
            start = pl.multiple_of(c * KCH, KCH)
            s_buf[hh, slot] = _dot(k_ref[k_head(hh), pl.ds(start, KCH), :], q_ref[hh])

        def pv(hh, c, slot):
            return _dot(vt_ref[c, v_rows(hh), :], p_buf[hh, slot])

        def step(c, slot, carry, first=False, last=False):
            new = []
            for hh in range(2):
                m, l, alpha = carry[hh]
                if not last:
                    qk(hh, c + 1, 1 - slot)
                if not first:
                    acc_ref[hh] = alpha * acc_ref[hh] + pv(hh, c - 1, 1 - slot)
                s = s_buf[hh, slot]
                m_new = jnp.max(s, axis=0, keepdims=True)
                if not first:
                    m_new = jnp.maximum(m, m_new)
                    alpha = jnp.exp2(m - m_new)
                p = jnp.exp2(s - m_new)
                psum = jnp.sum(p, axis=0, keepdims=True)
                l = psum if first else alpha * l + psum
                p_buf[hh, slot] = p.astype(BF)
                new.append((m_new, l, alpha))
            return tuple(new)

        zero = jnp.zeros((1, MQ), F32)
        for hh in range(2):
            qk(hh, 0, 0)
            acc_ref[hh] = jnp.zeros((MLA_V, MQ), F32)
        carry = step(0, 0, ((zero, zero, zero),) * 2, first=True)
        carry = step(1, 1, carry)

        def two_steps(i, cr):
            cr = step(2 * i, 0, cr)
            return step(2 * i + 1, 1, cr)

        carry = lax.fori_loop(1, (NCH - 1) // 2, two_steps, carry)
        carry = step(NCH - 1, 0, carry, last=True)
        outs = []
        for hh in range(2):
            _, l, alpha = carry[hh]
            acc = alpha * acc_ref[hh] + pv(hh, NCH - 1, 0)
            outs.append(acc / l)
        o_ref[...] = jnp.concatenate(outs, axis=0).T.astype(o_ref.dtype)


def _attn_call(qt, k, vt, *, k_per_head, v_per_head, name):
    if k_per_head:
        k_spec = pl.BlockSpec((2, NA, HEAD_PAD), lambda p, j: (p, 0, 0))
    else:
        k_spec = pl.BlockSpec((1, NA, HEAD_PAD), lambda p, j: (0, 0, 0))
    if v_per_head:
        v_spec = pl.BlockSpec((NCH, 2 * MLA_V, KCH), lambda p, j: (0, p, 0))
    else:
        v_spec = pl.BlockSpec((NCH, GQA_HEAD_DIM, KCH), lambda p, j: (0, p // (GQA_GROUP // 2), 0))
    return pl.pallas_call(
        partial(_attn_kernel, k_per_head=k_per_head, v_per_head=v_per_head),
        grid=(4, NA // MQ),
        in_specs=[pl.BlockSpec((2, HEAD_PAD, MQ), lambda p, j: (p, 0, j)), k_spec, v_spec],
        out_specs=pl.BlockSpec((MQ, 2 * MLA_V), lambda p, j: (j, p)),
        out_shape=jax.ShapeDtypeStruct((NA, 8 * MLA_V), BF),
        scratch_shapes=[pltpu.VMEM((2, 2, KCH, MQ), F32),
                        pltpu.VMEM((2, 2, KCH, MQ), BF),
                        pltpu.VMEM((2, MLA_V, MQ), F32)],
        compiler_params=pltpu.CompilerParams(dimension_semantics=("arbitrary", "arbitrary"),
                                             vmem_limit_bytes=VMEM_LIMIT),
        name=name,
    )(qt, k, vt)


def _merge_kernel(x_ref, yp_ref, yc_ref, yn_ref, gates_ref, om_ref, og_ref, mod_ref,
                  cw_ref, cb_ref, lng_ref, lnb_ref, wco_ref, wmo_ref, wgo_ref, wout_ref,
                  o_ref, buf_ref):
    i = pl.program_id(0)
    left_ok = i >= 2
    right_ok = jnp.logical_and(i >= 1, i < NT - 1)
    buf_ref[0:HALO_Y, :] = jnp.where(left_ok, yp_ref[...], 0.0)
    buf_ref[HALO_Y:HALO_Y + TM, :] = yc_ref[...]
    buf_ref[HALO_Y + TM:HALO_Y + TM + HALO_Y, :] = jnp.where(right_ok, yn_ref[...], 0.0)

    off = HALO_Y - CONV_WIDTH // 2
    acc = jnp.zeros((TM, CONV_DIM), F32) + cb_ref[...]
    for k in range(CONV_WIDTH):
        acc = acc + buf_ref[off + k:off + k + TM, :] * cw_ref[k:k + 1, :]
    mu = jnp.mean(acc, axis=-1, keepdims=True)
    xc = acc - mu
    var = jnp.mean(xc * xc, axis=-1, keepdims=True)
    ln = xc * lax.rsqrt(var + EPS) * lng_ref[...] + lnb_ref[...]
    act = (ln * _sigmoid(ln)).astype(BF)
    br_conv = _dot(act, wco_ref[...])
    br_mla = _dot(om_ref[...], wmo_ref[...])
    br_gqa = _dot(og_ref[...], wgo_ref[...])
    merged = (gates_ref[:, 0:D] * br_conv + gates_ref[:, D:2 * D] * br_mla
              + gates_ref[:, 2 * D:3 * D] * br_gqa)
    res = _dot(merged.astype(BF), wout_ref[...])
    o_ref[...] = x_ref[...] + mod_ref[0, 2:3, :] * res


def _merge_call(xall, y, gates, om, og, mod_l, cw, cb, lng, lnb, wco, wmo, wgo, wout):
    ny = NA // HALO_Y
    per = TM // HALO_Y
    in_specs = [
        pl.BlockSpec((TM, D), lambda i: (i, 0)),
        pl.BlockSpec((HALO_Y, CONV_DIM), lambda i: (jnp.maximum(i * per - 1, 0), 0)),
        pl.BlockSpec((TM, CONV_DIM), lambda i: (i, 0)),
        pl.BlockSpec((HALO_Y, CONV_DIM), lambda i: (jnp.minimum((i + 1) * per, ny - 1), 0)),
        pl.BlockSpec((TM, 3 * D), lambda i: (i, 0)),
        pl.BlockSpec((TM, 8 * MLA_V), lambda i: (i, 0)),
        pl.BlockSpec((TM, 8 * MLA_V), lambda i: (i, 0)),
        pl.BlockSpec((1, N_MOD, D), lambda i: (jnp.where(i == 0, 1, 0), 0, 0)),
        _const_spec((CONV_WIDTH, CONV_DIM)),
        _const_spec((1, CONV_DIM)),
        _const_spec((1, CONV_DIM)),
        _const_spec((1, CONV_DIM)),
        _const_spec((CONV_DIM, D)),
        _const_spec((8 * MLA_V, D)),
        _const_spec((8 * GQA_HEAD_DIM, D)),
        _const_spec((D, D)),
    ]
    return pl.pallas_call(
        _merge_kernel,
        grid=(NT,),
        in_specs=in_specs,
        out_specs=pl.BlockSpec((TM, D), lambda i: (i, 0)),
        out_shape=jax.ShapeDtypeStruct((NA, D), F32),
        scratch_shapes=[pltpu.VMEM((TM + 2 * HALO_Y, CONV_DIM), F32)],
        compiler_params=pltpu.CompilerParams(dimension_semantics=("arbitrary",),
                                             vmem_limit_bytes=VMEM_LIMIT),
        name="merge",
    )(xall, y, y, y, gates, om, og, mod_l, cw, cb, lng, lnb, wco, wmo, wgo, wout)


FF_CHUNK = D_FF // 2


def _ffn_kernel(xp_ref, xc_ref, xn_ref, mod_ref, gn_ref, wup_ref, dw_ref, db_ref, wdown_ref,
                o_ref, ua_ref, ug_ref):
    i = pl.program_id(0)
    left_ok = i >= 2
    right_ok = jnp.logical_and(i >= 1, i < NT - 1)
    xc = xc_ref[...]
    xh = jnp.concatenate([xp_ref[...], xc, xn_ref[...]], axis=0)
    shift = mod_ref[0, 3:4, :]
    scale = mod_ref[0, 4:5, :]
    h = xh * lax.rsqrt(jnp.mean(xh * xh, axis=-1, keepdims=True) + EPS) * gn_ref[...]
    hb = (h * (1.0 + scale) + shift).astype(BF)

    rows = TM + 2 * HALO_X
    row_id = lax.broadcasted_iota(jnp.int32, (rows, 1), 0)
    keep = jnp.logical_and(jnp.logical_or(row_id >= HALO_X, left_ok),
                           jnp.logical_or(row_id < HALO_X + TM, right_ok))

    def conv3(u_ref, col0, width):
        w = dw_ref[:, col0:col0 + width]
        return (u_ref[HALO_X - 1:HALO_X - 1 + TM, :] * w[0:1]
                + u_ref[HALO_X:HALO_X + TM, :] * w[1:2]
                + u_ref[HALO_X + 1:HALO_X + 1 + TM, :] * w[2:3]
                + db_ref[:, col0:col0 + width])

    acc = jnp.zeros((TM, D), F32)
    for c in range(D_FF // FF_CHUNK):
        a0 = c * FF_CHUNK
        g0 = D_FF + c * FF_CHUNK
        ua_ref[...] = jnp.where(keep, _dot(hb, wup_ref[:, a0:a0 + FF_CHUNK]), 0.0)
        ug_ref[...] = jnp.where(keep, _dot(hb, wup_ref[:, g0:g0 + FF_CHUNK]), 0.0)
        a = conv3(ua_ref, a0, FF_CHUNK)
        g = conv3(ug_ref, g0, FF_CHUNK)
        act = (g * _sigmoid(g) * a).astype(BF)
        acc = acc + _dot(act, wdown_ref[a0:a0 + FF_CHUNK, :])
    o_ref[...] = xc + mod_ref[0, 5:6, :] * acc


def _ffn_call(x1, mod_l, gn, wup, dw, db, wdown):
    nx = NA // HALO_X
    per = TM // HALO_X
    in_specs = [
        pl.BlockSpec((HALO_X, D), lambda i: (jnp.maximum(i * per - 1, 0), 0)),
        pl.BlockSpec((TM, D), lambda i: (i, 0)),
        pl.BlockSpec((HALO_X, D), lambda i: (jnp.minimum((i + 1) * per, nx - 1), 0)),
        pl.BlockSpec((1, N_MOD, D), lambda i: (jnp.where(i == 0, 1, 0), 0, 0)),
        _const_spec((1, D)),
        _const_spec((D, 2 * D_FF)),
        _const_spec((3, 2 * D_FF)),
        _const_spec((1, 2 * D_FF)),
        _const_spec((D_FF, D)),
    ]
    return pl.pallas_call(
        _ffn_kernel,
        grid=(NT,),
        in_specs=in_specs,
        out_specs=pl.BlockSpec((TM, D), lambda i: (i, 0)),
        out_shape=jax.ShapeDtypeStruct((NA, D), F32),
        scratch_shapes=[pltpu.VMEM((TM + 2 * HALO_X, FF_CHUNK), F32),
                        pltpu.VMEM((TM + 2 * HALO_X, FF_CHUNK), F32)],
        compiler_params=pltpu.CompilerParams(dimension_semantics=("arbitrary",),
                                             vmem_limit_bytes=VMEM_LIMIT),
        name="ffn",
    )(x1, x1, x1, mod_l, gn, wup, dw, db, wdown)


def _rope_tables(rot_dim):
    n_freq = rot_dim // 4
    inv = 1.0 / (ROPE_THETA ** (jnp.arange(n_freq, dtype=F32) / n_freq))
    pos = jnp.arange(SEQ, dtype=jnp.int32)
    ang_r = (pos // GRID_W).astype(F32)[None, :] * inv[:, None]
    ang_c = (pos % GRID_W).astype(F32)[None, :] * inv[:, None]
    ang = jnp.concatenate([ang_r, ang_r, ang_c, ang_c], axis=0)
    cos = jnp.concatenate([jnp.ones((rot_dim, CTX), F32), jnp.cos(ang)], axis=1)
    sin = jnp.concatenate([jnp.zeros((rot_dim, CTX), F32), jnp.sin(ang)], axis=1)
    return cos, sin


def _pad_col(g, n):
    return jnp.pad(g, (0, n - g.shape[0])).reshape(n, 1)


def kernel(x, c, ctx, c_ctx, w_mod, b_mod, g_norm1, g_norm2, w_in, b_gate, conv_dw_w, conv_dw_b, conv_ln_g, conv_ln_b, w_conv_out, g_q_a, w_q_b, g_kv_a, w_kv_b, g_mla_q, g_mla_k, w_mla_o, g_gqa_q, g_gqa_k, w_gqa_o, w_out, w_up, ffn_dw_w, ffn_dw_b, w_down):
    assert x.shape == (1, SEQ, D) and ctx.shape == (1, CTX, D)
    xall = jnp.concatenate([ctx[0], x[0]], axis=0)
    cond_t = jnp.stack([c[0], c_ctx], axis=1)
    mod = _mod_call(cond_t, w_mod, b_mod).reshape(DEPTH, 2, N_MOD, D)
    cosm, sinm = _rope_tables(MLA_ROPE)
    cosg, sing = _rope_tables(GQA_HEAD_DIM)

    o_qa = 2 * CONV_DIM
    o_kva = o_qa + MLA_Q_RANK
    o_kr = o_kva + MLA_KV_RANK
    o_gq = o_kr + MLA_ROPE
    o_gk = o_gq + GQA_HEADS * GQA_HEAD_DIM
    o_gv = o_gk + GQA_KV_HEADS * GQA_HEAD_DIM
    o_gate = o_gv + GQA_KV_HEADS * GQA_HEAD_DIM

    for l in range(DEPTH):
        wi = w_in[l]
        wmain = jnp.concatenate([wi[:, :o_qa], wi[:, o_gate:], wi[:, o_qa:o_kr]], axis=1).astype(BF)
        wfm = jnp.concatenate([wi[:, o_gq:o_gate], wi[:, o_kr:o_gq]], axis=1).T.astype(BF)
        wqb = jnp.pad(w_q_b[l].reshape(MLA_Q_RANK, MLA_HEADS, MLA_QK),
                      ((0, 0), (0, 0), (0, HEAD_PAD - MLA_QK)))
        wqb = wqb.reshape(MLA_Q_RANK, MLA_HEADS * HEAD_PAD).T.astype(BF)
        wkvb = w_kv_b[l].T.astype(BF)
        mod_l = mod[l]

        y, gates, qtm, km, vtm, qtg, kg, vtg = _proj_call(
            xall, mod_l, g_norm1[l].reshape(1, D), wmain, wfm, wqb, wkvb, b_gate[l].reshape(1, 3 * D),
            g_q_a[l].reshape(1, -1), g_kv_a[l].reshape(1, -1),
            _pad_col(g_mla_q[l], HEAD_PAD), _pad_col(g_mla_k[l], HEAD_PAD),
            g_gqa_q[l].reshape(-1, 1), g_gqa_k[l].reshape(-1, 1), cosm, sinm, cosg, sing)
        om = _attn_call(qtm, km, vtm, k_per_head=True, v_per_head=True, name="attn_mla")
        og = _attn_call(qtg, kg.reshape(1, NA, HEAD_PAD), vtg, k_per_head=False, v_per_head=False,
                        name="attn_gqa")
        x1 = _merge_call(xall, y, gates, om, og, mod_l, conv_dw_w[l], conv_dw_b[l].reshape(1, -1),
                         conv_ln_g[l].reshape(1, -1), conv_ln_b[l].reshape(1, -1),
                         w_conv_out[l].astype(BF), w_mla_o[l].astype(BF), w_gqa_o[l].astype(BF),
                         w_out[l].astype(BF))
        xall = _ffn_call(x1, mod_l, g_norm2[l].reshape(1, D), w_up[l].astype(BF), ffn_dw_w[l],
                         ffn_dw_b[l].reshape(1, -1), w_down[l].astype(BF))
    return xall[CTX:][None]
```

```python
from functools import partial

import jax
import jax.numpy as jnp
from jax import lax
from jax.experimental import pallas as pl
from jax.experimental.pallas import tpu as pltpu

D = 1024
SEQ = 16384
DEPTH = 4
GRID_W = 64
CTX = 256
NA = CTX + SEQ
ROPE_THETA = 10000.0
EPS = 1e-6
N_MOD = 6

CONV_DIM = 512
CONV_WIDTH = 31
MLA_HEADS = 8
MLA_Q_RANK = 384
MLA_KV_RANK = 256
MLA_NOPE = 64
MLA_ROPE = 32
MLA_V = 64
MLA_QK = MLA_NOPE + MLA_ROPE
GQA_HEADS = 8
GQA_KV_HEADS = 2
GQA_HEAD_DIM = 64
GQA_GROUP = GQA_HEADS // GQA_KV_HEADS
D_FF = 2816
LOG2E = 1.4426950408889634
MLA_SCALE = MLA_QK ** -0.5 * LOG2E
GQA_SCALE = GQA_HEAD_DIM ** -0.5 * LOG2E

HEAD_PAD = 128
TM = 256
NT = NA // TM
KC = 256
KCH = 1280
NCH = NA // KCH
KSUB = KCH // KC
MQ = 256
V_ROWS = 80
HALO_Y = 16
HALO_X = 8
VMEM_LIMIT = 56 * 1024 * 1024

BF = jnp.bfloat16
F32 = jnp.float32

C_GLU = 0
C_GATE = 2 * CONV_DIM
C_QA = C_GATE + 3 * D
C_KVA = C_QA + MLA_Q_RANK
W_MAIN = C_KVA + MLA_KV_RANK
R_GQ = 0
R_GK = GQA_HEADS * GQA_HEAD_DIM
R_GV = R_GK + GQA_KV_HEADS * GQA_HEAD_DIM
R_KR = R_GV + GQA_KV_HEADS * GQA_HEAD_DIM
W_FM = R_KR + MLA_ROPE


def _const_spec(shape):
    nd = len(shape)
    return pl.BlockSpec(shape, lambda *_: (0,) * nd, pipeline_mode=pl.Buffered(1))


def _dot(a, b):
    return jnp.dot(a, b, preferred_element_type=F32)


def _dot_nt(a, b):
    return lax.dot_general(a, b, (((1,), (1,)), ((), ())), preferred_element_type=F32)


def _sigmoid(x):
    return 1.0 / (1.0 + jnp.exp(-x))


def _rope_rows(x, cos, sin):
    q = x.shape[0] // 4
    x1, x2, x3, x4 = x[0:q], x[q:2 * q], x[2 * q:3 * q], x[3 * q:4 * q]
    rot = jnp.concatenate([-x2, x1, -x4, x3], axis=0)
    return x * cos + rot * sin


MOD_TN = 1536


def _mod_kernel(cond_ref, w_ref, b_ref, o_ref):
    c = cond_ref[...]
    s = c * _sigmoid(c)
    w = w_ref[0]
    b = b_ref[0]
    for j in range(2):
        o_ref[0, j:j + 1, :] = jnp.sum(s[:, j:j + 1] * w, axis=0, keepdims=True) + b


def _mod_call(cond_t, w_mod, b_mod):
    nl = w_mod.shape[0]
    ncol = N_MOD * D
    return pl.pallas_call(
        _mod_kernel,
        grid=(nl, ncol // MOD_TN),
        in_specs=[
            pl.BlockSpec((D, 2), lambda l, j: (0, 0)),
            pl.BlockSpec((1, D, MOD_TN), lambda l, j: (l, 0, j)),
            pl.BlockSpec((1, 1, MOD_TN), lambda l, j: (l, 0, j)),
        ],
        out_specs=pl.BlockSpec((1, 2, MOD_TN), lambda l, j: (l, 0, j)),
        out_shape=jax.ShapeDtypeStruct((nl, 2, ncol), F32),
        compiler_params=pltpu.CompilerParams(vmem_limit_bytes=VMEM_LIMIT),
        name="adaln_mod",
    )(cond_t, w_mod, b_mod.reshape(nl, 1, ncol))


def _proj_kernel(x_ref, mod_ref, gn_ref, wmain_ref, wfm_ref, wqb_ref, wkvb_ref, bgate_ref,
                 gqa_ref, gkva_ref, gmq_ref, gmk_ref, ggq_ref, ggk_ref,
                 cosm_ref, sinm_ref, cosg_ref, sing_ref,
                 y_ref, gates_ref, qtm_ref, km_ref, vtm_ref, qtg_ref, kg_ref, vtg_ref):
    x = x_ref[...]
    shift = mod_ref[0, 0:1, :]
    scale = mod_ref[0, 1:2, :]
    h = x * lax.rsqrt(jnp.mean(x * x, axis=-1, keepdims=True) + EPS) * gn_ref[...]
    hb = (h * (1.0 + scale) + shift).astype(BF)

    za = _dot(hb, wmain_ref[:, C_GLU:C_GLU + CONV_DIM])
    zg = _dot(hb, wmain_ref[:, C_GLU + CONV_DIM:C_GATE])
    y_ref[...] = za * _sigmoid(zg)
    zgate = _dot(hb, wmain_ref[:, C_GATE:C_QA])
    gates_ref[...] = _sigmoid(zgate + bgate_ref[...])

    zqa = _dot(hb, wmain_ref[:, C_QA:C_KVA])
    qa = (zqa * lax.rsqrt(jnp.mean(zqa * zqa, axis=-1, keepdims=True) + EPS) * gqa_ref[...]).astype(BF)
    zkva = _dot(hb, wmain_ref[:, C_KVA:W_MAIN])
    kva = (zkva * lax.rsqrt(jnp.mean(zkva * zkva, axis=-1, keepdims=True) + EPS) * gkva_ref[...]).astype(BF)
    qt = _dot_nt(wqb_ref[...], qa)
    kvt = _dot_nt(wkvb_ref[...], kva)
    fm = _dot_nt(wfm_ref[...], hb)

    cosm, sinm = cosm_ref[...], sinm_ref[...]
    cosg, sing = cosg_ref[...], sing_ref[...]
    gmq, gmk = gmq_ref[...], gmk_ref[...]
    ggq, ggk = ggq_ref[...], ggk_ref[...]
    kr = fm[R_KR:R_KR + MLA_ROPE]
    kr_ss = jnp.sum(kr * kr, axis=0, keepdims=True)
    zpad = jnp.zeros((HEAD_PAD - MLA_QK, TM), F32)
    ones_rows = jnp.ones((V_ROWS - MLA_V, TM), BF)

    for hd in range(MLA_HEADS):
        q = qt[hd * HEAD_PAD:(hd + 1) * HEAD_PAD]
        r = lax.rsqrt(jnp.sum(q * q, axis=0, keepdims=True) * (1.0 / MLA_QK) + EPS)
        qn = q * r * gmq
        qr = _rope_rows(qn[MLA_NOPE:MLA_QK], cosm, sinm)
        qfull = jnp.concatenate([qn[:MLA_NOPE], qr, zpad], axis=0) * MLA_SCALE
        qtm_ref[hd] = qfull.astype(BF)

        base = hd * (MLA_NOPE + MLA_V)
        kn = kvt[base:base + MLA_NOPE]
        vtm_ref[0, hd * V_ROWS:hd * V_ROWS + MLA_V, :] = kvt[base + MLA_NOPE:base + MLA_NOPE + MLA_V].astype(BF)
        vtm_ref[0, hd * V_ROWS + MLA_V:(hd + 1) * V_ROWS, :] = ones_rows
        rk = lax.rsqrt((jnp.sum(kn * kn, axis=0, keepdims=True) + kr_ss) * (1.0 / MLA_QK) + EPS)
        knn = kn * rk * gmk[:MLA_NOPE]
        krn = _rope_rows(kr * rk * gmk[MLA_NOPE:MLA_QK], cosm, sinm)
        kfull = jnp.concatenate([knn, krn, zpad], axis=0)
        km_ref[hd] = kfull.T.astype(BF)

    zhalf = jnp.zeros((GQA_HEAD_DIM, TM), F32)
    for hd in range(GQA_HEADS):
        q = fm[R_GQ + hd * GQA_HEAD_DIM:R_GQ + (hd + 1) * GQA_HEAD_DIM]
        r = lax.rsqrt(jnp.mean(q * q, axis=0, keepdims=True) + EPS)
        qn = _rope_rows(q * r * ggq, cosg, sing) * GQA_SCALE
        parts = [qn, zhalf] if hd // GQA_GROUP == 0 else [zhalf, qn]
        qtg_ref[hd] = jnp.concatenate(parts, axis=0).astype(BF)
    ks = []
    for g in range(GQA_KV_HEADS):
        k = fm[R_GK + g * GQA_HEAD_DIM:R_GK + (g + 1) * GQA_HEAD_DIM]
        r = lax.rsqrt(jnp.mean(k * k, axis=0, keepdims=True) + EPS)
        ks.append(_rope_rows(k * r * ggk, cosg, sing))
    kg_ref[...] = jnp.concatenate(ks, axis=0).T.astype(BF)
    for g in range(GQA_KV_HEADS):
        v = fm[R_GV + g * GQA_HEAD_DIM:R_GV + (g + 1) * GQA_HEAD_DIM]
        vtg_ref[0, g * V_ROWS:g * V_ROWS + GQA_HEAD_DIM, :] = v.astype(BF)
        vtg_ref[0, g * V_ROWS + GQA_HEAD_DIM:(g + 1) * V_ROWS, :] = ones_rows


def _proj_call(xall, mod_l, gn, wmain, wfm, wqb, wkvb, bgate, gqa, gkva, gmq, gmk, ggq, ggk,
               cosm, sinm, cosg, sing):
    def stream(i):
        return jnp.where(i == 0, 1, 0)

    in_specs = [
        pl.BlockSpec((TM, D), lambda i: (i, 0)),
        pl.BlockSpec((1, N_MOD, D), lambda i: (stream(i), 0, 0)),
        _const_spec((1, D)),
        _const_spec((D, W_MAIN)),
        _const_spec((W_FM, D)),
        _const_spec((MLA_HEADS * HEAD_PAD, MLA_Q_RANK)),
        _const_spec((MLA_HEADS * (MLA_NOPE + MLA_V), MLA_KV_RANK)),
        _const_spec((1, 3 * D)),
        _const_spec((1, MLA_Q_RANK)),
        _const_spec((1, MLA_KV_RANK)),
        _const_spec((HEAD_PAD, 1)),
        _const_spec((HEAD_PAD, 1)),
        _const_spec((GQA_HEAD_DIM, 1)),
        _const_spec((GQA_HEAD_DIM, 1)),
        pl.BlockSpec((MLA_ROPE, TM), lambda i: (0, i)),
        pl.BlockSpec((MLA_ROPE, TM), lambda i: (0, i)),
        pl.BlockSpec((GQA_HEAD_DIM, TM), lambda i: (0, i)),
        pl.BlockSpec((GQA_HEAD_DIM, TM), lambda i: (0, i)),
    ]
    out_specs = [
        pl.BlockSpec((TM, CONV_DIM), lambda i: (i, 0)),
        pl.BlockSpec((TM, 3 * D), lambda i: (i, 0)),
        pl.BlockSpec((MLA_HEADS, HEAD_PAD, TM), lambda i: (0, 0, i)),
        pl.BlockSpec((MLA_HEADS, TM, HEAD_PAD), lambda i: (0, i, 0)),
        pl.BlockSpec((1, MLA_HEADS * V_ROWS, KC), lambda i: (i // KSUB, 0, i % KSUB)),
        pl.BlockSpec((GQA_HEADS, HEAD_PAD, TM), lambda i: (0, 0, i)),
        pl.BlockSpec((TM, HEAD_PAD), lambda i: (i, 0)),
        pl.BlockSpec((1, GQA_KV_HEADS * V_ROWS, KC), lambda i: (i // KSUB, 0, i % KSUB)),
    ]
    out_shape = [
        jax.ShapeDtypeStruct((NA, CONV_DIM), F32),
        jax.ShapeDtypeStruct((NA, 3 * D), F32),
        jax.ShapeDtypeStruct((MLA_HEADS, HEAD_PAD, NA), BF),
        jax.ShapeDtypeStruct((MLA_HEADS, NA, HEAD_PAD), BF),
        jax.ShapeDtypeStruct((NCH, MLA_HEADS * V_ROWS, KCH), BF),
        jax.ShapeDtypeStruct((GQA_HEADS, HEAD_PAD, NA), BF),
        jax.ShapeDtypeStruct((NA, HEAD_PAD), BF),
        jax.ShapeDtypeStruct((NCH, GQA_KV_HEADS * V_ROWS, KCH), BF),
    ]
    return pl.pallas_call(
        _proj_kernel,
        grid=(NT,),
        in_specs=in_specs,
        out_specs=out_specs,
        out_shape=out_shape,
        compiler_params=pltpu.CompilerParams(dimension_semantics=("arbitrary",),
                                             vmem_limit_bytes=VMEM_LIMIT),
        name="proj",
    )(xall, mod_l, gn, wmain, wfm, wqb, wkvb, bgate, gqa, gkva, gmq, gmk, ggq, ggk,
      cosm, sinm, cosg, sing)


def _attn_kernel(q_ref, k_ref, vt_ref, o_ref, s_buf, p_buf, acc_ref, *, k_per_head, v_per_head):
    j = pl.program_id(1)

    def k_head(hh):
        return hh if k_per_head else 0

    def v_rows(hh):
        v0 = hh * V_ROWS if v_per_head else 0
        return slice(v0, v0 + V_ROWS)

    def finish(acc):
        return acc[0:MLA_V] / acc[MLA_V:MLA_V + 1]

    @pl.when(j == 0)
    def _():
        outs = []
        for hh in range(2):
            s = _dot(k_ref[k_head(hh), 0:CTX, :], q_ref[hh])
            p = jnp.exp2(s - jnp.max(s, axis=0, keepdims=True))
            outs.append(finish(_dot(vt_ref[0, v_rows(hh), 0:CTX], p.astype(BF))))
        o_ref[...] = jnp.concatenate(outs, axis=0).T.astype(o_ref.dtype)

    @pl.when(j > 0)
    def _():
        def qk(hh, c, slot):
            start = pl.multiple_of(c * KCH, KCH)
            s = _dot(k_ref[k_head(hh), pl.ds(start, KCH), :], q_ref[hh])
            s_buf[hh, slot] = s
            return jnp.max(s, axis=0, keepdims=True)

        def pv(hh, c, slot):
            return _dot(vt_ref[c, v_rows(hh), :], p_buf[hh, slot])

        def step(c, slot, carry, first=False, last=False):
            new = []
            for hh in range(2):
                m, alpha, mx = carry[hh]
                if first:
                    m_new, alpha_new = mx, alpha
                else:
                    m_new = jnp.maximum(m, mx)
                    alpha_new = jnp.exp2(m - m_new)
                p_buf[hh, slot] = jnp.exp2(s_buf[hh, slot] - m_new).astype(BF)
                new.append((m_new, alpha_new))
            out = []
            for hh in range(2):
                _, alpha, mx = carry[hh]
                mx_next = mx if last else qk(hh, c + 1, 1 - slot)
                if not first:
                    acc_ref[hh] = alpha * acc_ref[hh] + pv(hh, c - 1, 1 - slot)
                out.append(new[hh] + (mx_next,))
            return tuple(out)

        zero = jnp.zeros((1, MQ), F32)
        carry = []
        for hh in range(2):
            acc_ref[hh] = jnp.zeros((V_ROWS, MQ), F32)
            carry.append((zero, zero, qk(hh, 0, 0)))
        carry = tuple(carry)
        for c in range(NCH):
            carry = step(c, c % 2, carry, first=(c == 0), last=(c == NCH - 1))
        outs = []
        for hh in range(2):
            _, alpha, _ = carry[hh]
            outs.append(finish(alpha * acc_ref[hh] + pv(hh, NCH - 1, (NCH - 1) % 2)))
        o_ref[...] = jnp.concatenate(outs, axis=0).T.astype(o_ref.dtype)


def _attn_call(qt, k, vt, *, k_per_head, v_per_head, name):
    if k_per_head:
        k_spec = pl.BlockSpec((2, NA, HEAD_PAD), lambda p, j: (p, 0, 0))
    else:
        k_spec = pl.BlockSpec((1, NA, HEAD_PAD), lambda p, j: (0, 0, 0))
    if v_per_head:
        v_spec = pl.BlockSpec((NCH, 2 * V_ROWS, KCH), lambda p, j: (0, p, 0))
    else:
        v_spec = pl.BlockSpec((NCH, V_ROWS, KCH), lambda p, j: (0, p // (GQA_GROUP // 2), 0))
    return pl.pallas_call(
        partial(_attn_kernel, k_per_head=k_per_head, v_per_head=v_per_head),
        grid=(4, NA // MQ),
        in_specs=[pl.BlockSpec((2, HEAD_PAD, MQ), lambda p, j: (p, 0, j)), k_spec, v_spec],
        out_specs=pl.BlockSpec((MQ, 2 * MLA_V), lambda p, j: (j, p)),
        out_shape=jax.ShapeDtypeStruct((NA, 8 * MLA_V), BF),
        scratch_shapes=[pltpu.VMEM((2, 2, KCH, MQ), F32),
                        pltpu.VMEM((2, 2, KCH, MQ), BF),
                        pltpu.VMEM((2, V_ROWS, MQ), F32)],
        compiler_params=pltpu.CompilerParams(dimension_semantics=("arbitrary", "arbitrary"),
                                             vmem_limit_bytes=VMEM_LIMIT),
        name=name,
    )(qt, k, vt)


def _merge_kernel(x_ref, yp_ref, yc_ref, yn_ref, gates_ref, om_ref, og_ref, mod_ref,
                  cw_ref, cb_ref, lng_ref, lnb_ref, wco_ref, wmo_ref, wgo_ref, wout_ref,
                  o_ref, buf_ref):
    i = pl.program_id(0)
    left_ok = i >= 2
    right_ok = jnp.logical_and(i >= 1, i < NT - 1)
    buf_ref[0:HALO_Y, :] = jnp.where(left_ok, yp_ref[...], 0.0)
    buf_ref[HALO_Y:HALO_Y + TM, :] = yc_ref[...]
    buf_ref[HALO_Y + TM:HALO_Y + TM + HALO_Y, :] = jnp.where(right_ok, yn_ref[...], 0.0)

    off = HALO_Y - CONV_WIDTH // 2
    acc = jnp.zeros((TM, CONV_DIM), F32) + cb_ref[...]
    for k in range(CONV_WIDTH):
        acc = acc + buf_ref[off + k:off + k + TM, :] * cw_ref[k:k + 1, :]
    mu = jnp.mean(acc, axis=-1, keepdims=True)
    xc = acc - mu
    var = jnp.mean(xc * xc, axis=-1, keepdims=True)
    ln = xc * lax.rsqrt(var + EPS) * lng_ref[...] + lnb_ref[...]
    act = (ln * _sigmoid(ln)).astype(BF)
    br_conv = _dot(act, wco_ref[...])
    br_mla = _dot(om_ref[...], wmo_ref[...])
    br_gqa = _dot(og_ref[...], wgo_ref[...])
    merged = (gates_ref[:, 0:D] * br_conv + gates_ref[:, D:2 * D] * br_mla
              + gates_ref[:, 2 * D:3 * D] * br_gqa)
    res = _dot(merged.astype(BF), wout_ref[...])
    o_ref[...] = x_ref[...] + mod_ref[0, 2:3, :] * res


def _merge_call(xall, y, gates, om, og, mod_l, cw, cb, lng, lnb, wco, wmo, wgo, wout):
    ny = NA // HALO_Y
    per = TM // HALO_Y
    in_specs = [
        pl.BlockSpec((TM, D), lambda i: (i, 0)),
        pl.BlockSpec((HALO_Y, CONV_DIM), lambda i: (jnp.maximum(i * per - 1, 0), 0)),
        pl.BlockSpec((TM, CONV_DIM), lambda i: (i, 0)),
        pl.BlockSpec((HALO_Y, CONV_DIM), lambda i: (jnp.minimum((i + 1) * per, ny - 1), 0)),
        pl.BlockSpec((TM, 3 * D), lambda i: (i, 0)),
        pl.BlockSpec((TM, 8 * MLA_V), lambda i: (i, 0)),
        pl.BlockSpec((TM, 8 * MLA_V), lambda i: (i, 0)),
        pl.BlockSpec((1, N_MOD, D), lambda i: (jnp.where(i == 0, 1, 0), 0, 0)),
        _const_spec((CONV_WIDTH, CONV_DIM)),
        _const_spec((1, CONV_DIM)),
        _const_spec((1, CONV_DIM)),
        _const_spec((1, CONV_DIM)),
        _const_spec((CONV_DIM, D)),
        _const_spec((8 * MLA_V, D)),
        _const_spec((8 * GQA_HEAD_DIM, D)),
        _const_spec((D, D)),
    ]
    return pl.pallas_call(
        _merge_kernel,
        grid=(NT,),
        in_specs=in_specs,
        out_specs=pl.BlockSpec((TM, D), lambda i: (i, 0)),
        out_shape=jax.ShapeDtypeStruct((NA, D), F32),
        scratch_shapes=[pltpu.VMEM((TM + 2 * HALO_Y, CONV_DIM), F32)],
        compiler_params=pltpu.CompilerParams(dimension_semantics=("arbitrary",),
                                             vmem_limit_bytes=VMEM_LIMIT),
        name="merge",
    )(xall, y, y, y, gates, om, og, mod_l, cw, cb, lng, lnb, wco, wmo, wgo, wout)


FF_CHUNK = D_FF // 2


def _ffn_kernel(xp_ref, xc_ref, xn_ref, mod_ref, gn_ref, wup_ref, dw_ref, db_ref, wdown_ref,
                o_ref, ua_ref, ug_ref):
    i = pl.program_id(0)
    left_ok = i >= 2
    right_ok = jnp.logical_and(i >= 1, i < NT - 1)
    xc = xc_ref[...]
    xh = jnp.concatenate([xp_ref[...], xc, xn_ref[...]], axis=0)
    shift = mod_ref[0, 3:4, :]
    scale = mod_ref[0, 4:5, :]
    h = xh * lax.rsqrt(jnp.mean(xh * xh, axis=-1, keepdims=True) + EPS) * gn_ref[...]
    hb = (h * (1.0 + scale) + shift).astype(BF)

    rows = TM + 2 * HALO_X
    row_id = lax.broadcasted_iota(jnp.int32, (rows, 1), 0)
    keep = jnp.logical_and(jnp.logical_or(row_id >= HALO_X, left_ok),
                           jnp.logical_or(row_id < HALO_X + TM, right_ok))

    def conv3(u_ref, col0, width):
        w = dw_ref[:, col0:col0 + width]
        return (u_ref[HALO_X - 1:HALO_X - 1 + TM, :] * w[0:1]
                + u_ref[HALO_X:HALO_X + TM, :] * w[1:2]
                + u_ref[HALO_X + 1:HALO_X + 1 + TM, :] * w[2:3]
                + db_ref[:, col0:col0 + width])

    acc = jnp.zeros((TM, D), F32)
    for c in range(D_FF // FF_CHUNK):
        a0 = c * FF_CHUNK
        g0 = D_FF + c * FF_CHUNK
        ua_ref[...] = jnp.where(keep, _dot(hb, wup_ref[:, a0:a0 + FF_CHUNK]), 0.0)
        ug_ref[...] = jnp.where(keep, _dot(hb, wup_ref[:, g0:g0 + FF_CHUNK]), 0.0)
        a = conv3(ua_ref, a0, FF_CHUNK)
        g = conv3(ug_ref, g0, FF_CHUNK)
        act = (g * _sigmoid(g) * a).astype(BF)
        acc = acc + _dot(act, wdown_ref[a0:a0 + FF_CHUNK, :])
    o_ref[...] = xc + mod_ref[0, 5:6, :] * acc


def _ffn_call(x1, mod_l, gn, wup, dw, db, wdown):
    nx = NA // HALO_X
    per = TM // HALO_X
    in_specs = [
        pl.BlockSpec((HALO_X, D), lambda i: (jnp.maximum(i * per - 1, 0), 0)),
        pl.BlockSpec((TM, D), lambda i: (i, 0)),
        pl.BlockSpec((HALO_X, D), lambda i: (jnp.minimum((i + 1) * per, nx - 1), 0)),
        pl.BlockSpec((1, N_MOD, D), lambda i: (jnp.where(i == 0, 1, 0), 0, 0)),
        _const_spec((1, D)),
        _const_spec((D, 2 * D_FF)),
        _const_spec((3, 2 * D_FF)),
        _const_spec((1, 2 * D_FF)),
        _const_spec((D_FF, D)),
    ]
    return pl.pallas_call(
        _ffn_kernel,
        grid=(NT,),
        in_specs=in_specs,
        out_specs=pl.BlockSpec((TM, D), lambda i: (i, 0)),
        out_shape=jax.ShapeDtypeStruct((NA, D), F32),
        scratch_shapes=[pltpu.VMEM((TM + 2 * HALO_X, FF_CHUNK), F32),
                        pltpu.VMEM((TM + 2 * HALO_X, FF_CHUNK), F32)],
        compiler_params=pltpu.CompilerParams(dimension_semantics=("arbitrary",),
                                             vmem_limit_bytes=VMEM_LIMIT),
        name="ffn",
    )(x1, x1, x1, mod_l, gn, wup, dw, db, wdown)


def _rope_tables(rot_dim):
    n_freq = rot_dim // 4
    inv = 1.0 / (ROPE_THETA ** (jnp.arange(n_freq, dtype=F32) / n_freq))
    pos = jnp.arange(SEQ, dtype=jnp.int32)
    ang_r = (pos // GRID_W).astype(F32)[None, :] * inv[:, None]
    ang_c = (pos % GRID_W).astype(F32)[None, :] * inv[:, None]
    ang = jnp.concatenate([ang_r, ang_r, ang_c, ang_c], axis=0)
    cos = jnp.concatenate([jnp.ones((rot_dim, CTX), F32), jnp.cos(ang)], axis=1)
    sin = jnp.concatenate([jnp.zeros((rot_dim, CTX), F32), jnp.sin(ang)], axis=1)
    return cos, sin


def _pad_col(g, n):
    return jnp.pad(g, (0, n - g.shape[0])).reshape(n, 1)


def kernel(x, c, ctx, c_ctx, w_mod, b_mod, g_norm1, g_norm2, w_in, b_gate, conv_dw_w, conv_dw_b, conv_ln_g, conv_ln_b, w_conv_out, g_q_a, w_q_b, g_kv_a, w_kv_b, g_mla_q, g_mla_k, w_mla_o, g_gqa_q, g_gqa_k, w_gqa_o, w_out, w_up, ffn_dw_w, ffn_dw_b, w_down):
    assert x.shape == (1, SEQ, D) and ctx.shape == (1, CTX, D)
    xall = jnp.concatenate([ctx[0], x[0]], axis=0)
    cond_t = jnp.stack([c[0], c_ctx], axis=1)
    mod = _mod_call(cond_t, w_mod, b_mod).reshape(DEPTH, 2, N_MOD, D)
    cosm, sinm = _rope_tables(MLA_ROPE)
    cosg, sing = _rope_tables(GQA_HEAD_DIM)

    o_qa = 2 * CONV_DIM
    o_kva = o_qa + MLA_Q_RANK
    o_kr = o_kva + MLA_KV_RANK
    o_gq = o_kr + MLA_ROPE
    o_gk = o_gq + GQA_HEADS * GQA_HEAD_DIM
    o_gv = o_gk + GQA_KV_HEADS * GQA_HEAD_DIM
    o_gate = o_gv + GQA_KV_HEADS * GQA_HEAD_DIM

    for l in range(DEPTH):
        wi = w_in[l]
        wmain = jnp.concatenate([wi[:, :o_qa], wi[:, o_gate:], wi[:, o_qa:o_kr]], axis=1).astype(BF)
        wfm = jnp.concatenate([wi[:, o_gq:o_gate], wi[:, o_kr:o_gq]], axis=1).T.astype(BF)
        wqb = jnp.pad(w_q_b[l].reshape(MLA_Q_RANK, MLA_HEADS, MLA_QK),
                      ((0, 0), (0, 0), (0, HEAD_PAD - MLA_QK)))
        wqb = wqb.reshape(MLA_Q_RANK, MLA_HEADS * HEAD_PAD).T.astype(BF)
        wkvb = w_kv_b[l].T.astype(BF)
        mod_l = mod[l]

        y, gates, qtm, km, vtm, qtg, kg, vtg = _proj_call(
            xall, mod_l, g_norm1[l].reshape(1, D), wmain, wfm, wqb, wkvb, b_gate[l].reshape(1, 3 * D),
            g_q_a[l].reshape(1, -1), g_kv_a[l].reshape(1, -1),
            _pad_col(g_mla_q[l], HEAD_PAD), _pad_col(g_mla_k[l], HEAD_PAD),
            g_gqa_q[l].reshape(-1, 1), g_gqa_k[l].reshape(-1, 1), cosm, sinm, cosg, sing)
        om = _attn_call(qtm, km, vtm, k_per_head=True, v_per_head=True, name="attn_mla")
        og = _attn_call(qtg, kg.reshape(1, NA, HEAD_PAD), vtg, k_per_head=False, v_per_head=False,
                        name="attn_gqa")
        x1 = _merge_call(xall, y, gates, om, og, mod_l, conv_dw_w[l], conv_dw_b[l].reshape(1, -1),
                         conv_ln_g[l].reshape(1, -1), conv_ln_b[l].reshape(1, -1),
                         w_conv_out[l].astype(BF), w_mla_o[l].astype(BF), w_gqa_o[l].astype(BF),
                         w_out[l].astype(BF))
        xall = _ffn_call(x1, mod_l, g_norm2[l].reshape(1, D), w_up[l].astype(BF), ffn_dw_w[l],
                         ffn_dw_b[l].reshape(1, -1), w_down[l].astype(BF))
    return xall[CTX:][None]
```

```python
from functools import partial

import jax
import jax.numpy as jnp
from jax import lax
from jax.experimental import pallas as pl
from jax.experimental.pallas import tpu as pltpu

D = 1024
SEQ = 16384
DEPTH = 4
GRID_W = 64
CTX = 256
NA = CTX + SEQ
ROPE_THETA = 10000.0
EPS = 1e-6
N_MOD = 6

CONV_DIM = 512
CONV_WIDTH = 31
MLA_HEADS = 8
MLA_Q_RANK = 384
MLA_KV_RANK = 256
MLA_NOPE = 64
MLA_ROPE = 32
MLA_V = 64
MLA_QK = MLA_NOPE + MLA_ROPE
GQA_HEADS = 8
GQA_KV_HEADS = 2
GQA_HEAD_DIM = 64
GQA_GROUP = GQA_HEADS // GQA_KV_HEADS
D_FF = 2816
LOG2E = 1.4426950408889634
MLA_SCALE = MLA_QK ** -0.5 * LOG2E
GQA_SCALE = GQA_HEAD_DIM ** -0.5 * LOG2E

HEAD_PAD = 128
TM = 256
NT = NA // TM
KC = 256
MQ = 256
DEN_FLOOR = 2.0 ** -60
P_SLOTS = 2
V_ROWS = 80
HALO_Y = 16
HALO_X = 8
VMEM_LIMIT = 56 * 1024 * 1024

BF = jnp.bfloat16
F32 = jnp.float32

C_GLU = 0
C_GATE = 2 * CONV_DIM
C_QA = C_GATE + 3 * D
C_KVA = C_QA + MLA_Q_RANK
W_MAIN = C_KVA + MLA_KV_RANK
R_GQ = 0
R_GK = GQA_HEADS * GQA_HEAD_DIM
R_GV = R_GK + GQA_KV_HEADS * GQA_HEAD_DIM
R_KR = R_GV + GQA_KV_HEADS * GQA_HEAD_DIM
W_FM = R_KR + MLA_ROPE


def _const_spec(shape):
    nd = len(shape)
    return pl.BlockSpec(shape, lambda *_: (0,) * nd, pipeline_mode=pl.Buffered(1))


def _dot(a, b):
    return jnp.dot(a, b, preferred_element_type=F32)


def _dot_nt(a, b):
    return lax.dot_general(a, b, (((1,), (1,)), ((), ())), preferred_element_type=F32)


def _sigmoid(x):
    return 1.0 / (1.0 + jnp.exp(-x))


def _sq_norm_as_stored(kt):
    kq = kt.astype(BF).astype(F32)
    return jnp.sum(kq * kq, axis=0, keepdims=True)


def _rope_rows(x, cos, sin):
    q = x.shape[0] // 4
    x1, x2, x3, x4 = x[0:q], x[q:2 * q], x[2 * q:3 * q], x[3 * q:4 * q]
    rot = jnp.concatenate([-x2, x1, -x4, x3], axis=0)
    return x * cos + rot * sin


MOD_TN = 1536


def _mod_kernel(cond_ref, w_ref, b_ref, o_ref):
    c = cond_ref[...]
    s = c * _sigmoid(c)
    w = w_ref[0]
    b = b_ref[0]
    for j in range(2):
        o_ref[0, j:j + 1, :] = jnp.sum(s[:, j:j + 1] * w, axis=0, keepdims=True) + b


def _mod_call(cond_t, w_mod, b_mod):
    nl = w_mod.shape[0]
    ncol = N_MOD * D
    return pl.pallas_call(
        _mod_kernel,
        grid=(nl, ncol // MOD_TN),
        in_specs=[
            pl.BlockSpec((D, 2), lambda l, j: (0, 0)),
            pl.BlockSpec((1, D, MOD_TN), lambda l, j: (l, 0, j)),
            pl.BlockSpec((1, 1, MOD_TN), lambda l, j: (l, 0, j)),
        ],
        out_specs=pl.BlockSpec((1, 2, MOD_TN), lambda l, j: (l, 0, j)),
        out_shape=jax.ShapeDtypeStruct((nl, 2, ncol), F32),
        compiler_params=pltpu.CompilerParams(vmem_limit_bytes=VMEM_LIMIT),
        name="adaln_mod",
    )(cond_t, w_mod, b_mod.reshape(nl, 1, ncol))


def _proj_kernel(x_ref, mod_ref, gn_ref, wmain_ref, wfm_ref, wqb_ref, wkvb_ref, bgate_ref,
                 gqa_ref, gkva_ref, gmq_ref, gmk_ref, ggq_ref, ggk_ref,
                 cosm_ref, sinm_ref, cosg_ref, sing_ref,
                 y_ref, gates_ref, qtm_ref, km_ref, vtm_ref, knm_ref, qtg_ref, kg_ref, vtg_ref, kng_ref):
    x = x_ref[...]
    shift = mod_ref[0, 0:1, :]
    scale = mod_ref[0, 1:2, :]
    h = x * lax.rsqrt(jnp.mean(x * x, axis=-1, keepdims=True) + EPS) * gn_ref[...]
    hb = (h * (1.0 + scale) + shift).astype(BF)

    za = _dot(hb, wmain_ref[:, C_GLU:C_GLU + CONV_DIM])
    zg = _dot(hb, wmain_ref[:, C_GLU + CONV_DIM:C_GATE])
    y_ref[...] = za * _sigmoid(zg)
    zgate = _dot(hb, wmain_ref[:, C_GATE:C_QA])
    gates_ref[...] = _sigmoid(zgate + bgate_ref[...])

    zqa = _dot(hb, wmain_ref[:, C_QA:C_KVA])
    qa = (zqa * lax.rsqrt(jnp.mean(zqa * zqa, axis=-1, keepdims=True) + EPS) * gqa_ref[...]).astype(BF)
    zkva = _dot(hb, wmain_ref[:, C_KVA:W_MAIN])
    kva = (zkva * lax.rsqrt(jnp.mean(zkva * zkva, axis=-1, keepdims=True) + EPS) * gkva_ref[...]).astype(BF)
    qt = _dot_nt(wqb_ref[...], qa)
    kvt = _dot_nt(wkvb_ref[...], kva)
    fm = _dot_nt(wfm_ref[...], hb)

    cosm, sinm = cosm_ref[...], sinm_ref[...]
    cosg, sing = cosg_ref[...], sing_ref[...]
    gmq, gmk = gmq_ref[...], gmk_ref[...]
    ggq, ggk = ggq_ref[...], ggk_ref[...]
    kr = fm[R_KR:R_KR + MLA_ROPE]
    kr_ss = jnp.sum(kr * kr, axis=0, keepdims=True)
    zpad = jnp.zeros((HEAD_PAD - MLA_QK, TM), F32)
    ones_rows = jnp.ones((V_ROWS - MLA_V, TM), BF)

    for hd in range(MLA_HEADS):
        q = qt[hd * HEAD_PAD:(hd + 1) * HEAD_PAD]
        r = lax.rsqrt(jnp.sum(q * q, axis=0, keepdims=True) * (1.0 / MLA_QK) + EPS)
        qn = q * r * gmq
        qr = _rope_rows(qn[MLA_NOPE:MLA_QK], cosm, sinm)
        qfull = jnp.concatenate([qn[:MLA_NOPE], qr, zpad], axis=0) * MLA_SCALE
        qtm_ref[hd] = qfull.astype(BF)

        base = hd * (MLA_NOPE + MLA_V)
        kn = kvt[base:base + MLA_NOPE]
        vtm_ref[0, hd * V_ROWS:hd * V_ROWS + MLA_V, :] = kvt[base + MLA_NOPE:base + MLA_NOPE + MLA_V].astype(BF)
        vtm_ref[0, hd * V_ROWS + MLA_V:(hd + 1) * V_ROWS, :] = ones_rows
        rk = lax.rsqrt((jnp.sum(kn * kn, axis=0, keepdims=True) + kr_ss) * (1.0 / MLA_QK) + EPS)
        knn = kn * rk * gmk[:MLA_NOPE]
        krn = _rope_rows(kr * rk * gmk[MLA_NOPE:MLA_QK], cosm, sinm)
        kfull = jnp.concatenate([knn, krn, zpad], axis=0)
        km_ref[hd] = kfull.T.astype(BF)
        knm_ref[hd:hd + 1, :] = _sq_norm_as_stored(kfull)

    zhalf = jnp.zeros((GQA_HEAD_DIM, TM), F32)
    for hd in range(GQA_HEADS):
        q = fm[R_GQ + hd * GQA_HEAD_DIM:R_GQ + (hd + 1) * GQA_HEAD_DIM]
        r = lax.rsqrt(jnp.mean(q * q, axis=0, keepdims=True) + EPS)
        qn = _rope_rows(q * r * ggq, cosg, sing) * GQA_SCALE
        parts = [qn, zhalf] if hd // GQA_GROUP == 0 else [zhalf, qn]
        qtg_ref[hd] = jnp.concatenate(parts, axis=0).astype(BF)
    ks = []
    for g in range(GQA_KV_HEADS):
        k = fm[R_GK + g * GQA_HEAD_DIM:R_GK + (g + 1) * GQA_HEAD_DIM]
        r = lax.rsqrt(jnp.mean(k * k, axis=0, keepdims=True) + EPS)
        ks.append(_rope_rows(k * r * ggk, cosg, sing))
        kng_ref[g:g + 1, :] = _sq_norm_as_stored(ks[g])
    kg_ref[...] = jnp.concatenate(ks, axis=0).T.astype(BF)
    for g in range(GQA_KV_HEADS):
        v = fm[R_GV + g * GQA_HEAD_DIM:R_GV + (g + 1) * GQA_HEAD_DIM]
        vtg_ref[0, g * V_ROWS:g * V_ROWS + GQA_HEAD_DIM, :] = v.astype(BF)
        vtg_ref[0, g * V_ROWS + GQA_HEAD_DIM:(g + 1) * V_ROWS, :] = ones_rows


def _proj_call(xall, mod_l, gn, wmain, wfm, wqb, wkvb, bgate, gqa, gkva, gmq, gmk, ggq, ggk,
               cosm, sinm, cosg, sing):
    def stream(i):
        return jnp.where(i == 0, 1, 0)

    in_specs = [
        pl.BlockSpec((TM, D), lambda i: (i, 0)),
        pl.BlockSpec((1, N_MOD, D), lambda i: (stream(i), 0, 0)),
        _const_spec((1, D)),
        _const_spec((D, W_MAIN)),
        _const_spec((W_FM, D)),
        _const_spec((MLA_HEADS * HEAD_PAD, MLA_Q_RANK)),
        _const_spec((MLA_HEADS * (MLA_NOPE + MLA_V), MLA_KV_RANK)),
        _const_spec((1, 3 * D)),
        _const_spec((1, MLA_Q_RANK)),
        _const_spec((1, MLA_KV_RANK)),
        _const_spec((HEAD_PAD, 1)),
        _const_spec((HEAD_PAD, 1)),
        _const_spec((GQA_HEAD_DIM, 1)),
        _const_spec((GQA_HEAD_DIM, 1)),
        pl.BlockSpec((MLA_ROPE, TM), lambda i: (0, i)),
        pl.BlockSpec((MLA_ROPE, TM), lambda i: (0, i)),
        pl.BlockSpec((GQA_HEAD_DIM, TM), lambda i: (0, i)),
        pl.BlockSpec((GQA_HEAD_DIM, TM), lambda i: (0, i)),
    ]
    out_specs = [
        pl.BlockSpec((TM, CONV_DIM), lambda i: (i, 0)),
        pl.BlockSpec((TM, 3 * D), lambda i: (i, 0)),
        pl.BlockSpec((MLA_HEADS, HEAD_PAD, TM), lambda i: (0, 0, i)),
        pl.BlockSpec((MLA_HEADS, TM, HEAD_PAD), lambda i: (0, i, 0)),
        pl.BlockSpec((1, MLA_HEADS * V_ROWS, KC), lambda i: (i, 0, 0)),
        pl.BlockSpec((MLA_HEADS, TM), lambda i: (0, i)),
        pl.BlockSpec((GQA_HEADS, HEAD_PAD, TM), lambda i: (0, 0, i)),
        pl.BlockSpec((TM, HEAD_PAD), lambda i: (i, 0)),
        pl.BlockSpec((1, GQA_KV_HEADS * V_ROWS, KC), lambda i: (i, 0, 0)),
        pl.BlockSpec((GQA_KV_HEADS, TM), lambda i: (0, i)),
    ]
    out_shape = [
        jax.ShapeDtypeStruct((NA, CONV_DIM), F32),
        jax.ShapeDtypeStruct((NA, 3 * D), F32),
        jax.ShapeDtypeStruct((MLA_HEADS, HEAD_PAD, NA), BF),
        jax.ShapeDtypeStruct((MLA_HEADS, NA, HEAD_PAD), BF),
        jax.ShapeDtypeStruct((NT, MLA_HEADS * V_ROWS, KC), BF),
        jax.ShapeDtypeStruct((MLA_HEADS, NA), F32),
        jax.ShapeDtypeStruct((GQA_HEADS, HEAD_PAD, NA), BF),
        jax.ShapeDtypeStruct((NA, HEAD_PAD), BF),
        jax.ShapeDtypeStruct((NT, GQA_KV_HEADS * V_ROWS, KC), BF),
        jax.ShapeDtypeStruct((GQA_KV_HEADS, NA), F32),
    ]
    return pl.pallas_call(
        _proj_kernel,
        grid=(NT,),
        in_specs=in_specs,
        out_specs=out_specs,
        out_shape=out_shape,
        compiler_params=pltpu.CompilerParams(dimension_semantics=("arbitrary",),
                                             vmem_limit_bytes=VMEM_LIMIT),
        name="proj",
    )(xall, mod_l, gn, wmain, wfm, wqb, wkvb, bgate, gqa, gkva, gmq, gmk, ggq, ggk,
      cosm, sinm, cosg, sing)


def _attn_kernel(q_ref, k_ref, vt_ref, kn_ref, o_ref, p_buf, *, k_per_head, v_per_head):
    j = pl.program_id(1)

    def k_head(hh):
        return hh if k_per_head else 0

    def v_rows(hh):
        v0 = hh * V_ROWS if v_per_head else 0
        return slice(v0, v0 + V_ROWS)

    def finish(acc):
        return acc[0:MLA_V] / acc[MLA_V:MLA_V + 1]

    def write(outs):
        o_ref[...] = jnp.concatenate(outs, axis=0).T.astype(o_ref.dtype)

    def scores(hh, c):
        start = pl.multiple_of(c * KC, KC)
        return _dot(k_ref[k_head(hh), pl.ds(start, KC), :], q_ref[hh])

    def pv(hh, c, p):
        return _dot(vt_ref[c, v_rows(hh), :], p.astype(BF))

    def online(hh, n_chunks):
        s = scores(hh, 0)
        m = jnp.max(s, axis=0, keepdims=True)
        acc = pv(hh, 0, jnp.exp2(s - m))

        def body(c, carry):
            m, acc = carry
            s = scores(hh, c)
            m_new = jnp.maximum(m, jnp.max(s, axis=0, keepdims=True))
            return m_new, jnp.exp2(m - m_new) * acc + pv(hh, c, jnp.exp2(s - m_new))

        _, acc = lax.fori_loop(1, n_chunks, body, (m, acc))
        return finish(acc)

    @pl.when(j == 0)
    def _():
        write([online(hh, 1) for hh in range(2)])

    @pl.when(j > 0)
    def _():
        shift = []
        for hh in range(2):
            qf = q_ref[hh].astype(F32)
            qn2 = jnp.sum(qf * qf, axis=0, keepdims=True)
            kn2 = jnp.max(jnp.max(kn_ref[k_head(hh)], axis=0, keepdims=True), axis=1, keepdims=True)
            shift.append(jnp.sqrt(qn2 * kn2))
        acc = [jnp.zeros((V_ROWS, MQ), F32)] * 2
        for c in range(NT + 1):
            if c < NT:
                for hh in range(2):
                    s = _dot(k_ref[k_head(hh), c * KC:(c + 1) * KC, :], q_ref[hh])
                    p_buf[hh, c % P_SLOTS] = jnp.exp2(s - shift[hh]).astype(BF)
            if c >= 1:
                for hh in range(2):
                    acc[hh] = acc[hh] + _dot(vt_ref[c - 1, v_rows(hh), :], p_buf[hh, (c - 1) % P_SLOTS])
        write([finish(a) for a in acc])
        den_min = jnp.min(jnp.minimum(acc[0][MLA_V:MLA_V + 1], acc[1][MLA_V:MLA_V + 1]))

        @pl.when(jnp.logical_not(den_min >= DEN_FLOOR))
        def _():
            write([online(hh, NT) for hh in range(2)])


def _attn_call(qt, k, vt, kn, *, k_per_head, v_per_head, name):
    if k_per_head:
        k_spec = pl.BlockSpec((2, NA, HEAD_PAD), lambda p, j: (p, 0, 0))
        kn_spec = pl.BlockSpec((2, NA // 128, 128), lambda p, j: (p, 0, 0))
    else:
        k_spec = pl.BlockSpec((1, NA, HEAD_PAD), lambda p, j: (0, 0, 0))
        kn_spec = pl.BlockSpec((1, NA // 128, 128), lambda p, j: (p // (GQA_GROUP // 2), 0, 0))
    if v_per_head:
        v_spec = pl.BlockSpec((NT, 2 * V_ROWS, KC), lambda p, j: (0, p, 0))
    else:
        v_spec = pl.BlockSpec((NT, V_ROWS, KC), lambda p, j: (0, p // (GQA_GROUP // 2), 0))
    return pl.pallas_call(
        partial(_attn_kernel, k_per_head=k_per_head, v_per_head=v_per_head),
        grid=(4, NA // MQ),
        in_specs=[pl.BlockSpec((2, HEAD_PAD, MQ), lambda p, j: (p, 0, j)), k_spec, v_spec, kn_spec],
        out_specs=pl.BlockSpec((MQ, 2 * MLA_V), lambda p, j: (j, p)),
        out_shape=jax.ShapeDtypeStruct((NA, 8 * MLA_V), BF),
        scratch_shapes=[pltpu.VMEM((2, P_SLOTS, KC, MQ), BF)],
        compiler_params=pltpu.CompilerParams(dimension_semantics=("arbitrary", "arbitrary"),
                                             vmem_limit_bytes=VMEM_LIMIT),
        name=name,
    )(qt, k, vt, kn)


def _merge_kernel(x_ref, yp_ref, yc_ref, yn_ref, gates_ref, om_ref, og_ref, mod_ref,
                  cw_ref, cb_ref, lng_ref, lnb_ref, wco_ref, wmo_ref, wgo_ref, wout_ref,
                  o_ref, buf_ref):
    i = pl.program_id(0)
    left_ok = i >= 2
    right_ok = jnp.logical_and(i >= 1, i < NT - 1)
    buf_ref[0:HALO_Y, :] = jnp.where(left_ok, yp_ref[...], 0.0)
    buf_ref[HALO_Y:HALO_Y + TM, :] = yc_ref[...]
    buf_ref[HALO_Y + TM:HALO_Y + TM + HALO_Y, :] = jnp.where(right_ok, yn_ref[...], 0.0)

    off = HALO_Y - CONV_WIDTH // 2
    acc = jnp.zeros((TM, CONV_DIM), F32) + cb_ref[...]
    for k in range(CONV_WIDTH):
        acc = acc + buf_ref[off + k:off + k + TM, :] * cw_ref[k:k + 1, :]
    mu = jnp.mean(acc, axis=-1, keepdims=True)
    xc = acc - mu
    var = jnp.mean(xc * xc, axis=-1, keepdims=True)
    ln = xc * lax.rsqrt(var + EPS) * lng_ref[...] + lnb_ref[...]
    act = (ln * _sigmoid(ln)).astype(BF)
    br_conv = _dot(act, wco_ref[...])
    br_mla = _dot(om_ref[...], wmo_ref[...])
    br_gqa = _dot(og_ref[...], wgo_ref[...])
    merged = (gates_ref[:, 0:D] * br_conv + gates_ref[:, D:2 * D] * br_mla
              + gates_ref[:, 2 * D:3 * D] * br_gqa)
    res = _dot(merged.astype(BF), wout_ref[...])
    o_ref[...] = x_ref[...] + mod_ref[0, 2:3, :] * res


def _merge_call(xall, y, gates, om, og, mod_l, cw, cb, lng, lnb, wco, wmo, wgo, wout):
    ny = NA // HALO_Y
    per = TM // HALO_Y
    in_specs = [
        pl.BlockSpec((TM, D), lambda i: (i, 0)),
        pl.BlockSpec((HALO_Y, CONV_DIM), lambda i: (jnp.maximum(i * per - 1, 0), 0)),
        pl.BlockSpec((TM, CONV_DIM), lambda i: (i, 0)),
        pl.BlockSpec((HALO_Y, CONV_DIM), lambda i: (jnp.minimum((i + 1) * per, ny - 1), 0)),
        pl.BlockSpec((TM, 3 * D), lambda i: (i, 0)),
        pl.BlockSpec((TM, 8 * MLA_V), lambda i: (i, 0)),
        pl.BlockSpec((TM, 8 * MLA_V), lambda i: (i, 0)),
        pl.BlockSpec((1, N_MOD, D), lambda i: (jnp.where(i == 0, 1, 0), 0, 0)),
        _const_spec((CONV_WIDTH, CONV_DIM)),
        _const_spec((1, CONV_DIM)),
        _const_spec((1, CONV_DIM)),
        _const_spec((1, CONV_DIM)),
        _const_spec((CONV_DIM, D)),
        _const_spec((8 * MLA_V, D)),
        _const_spec((8 * GQA_HEAD_DIM, D)),
        _const_spec((D, D)),
    ]
    return pl.pallas_call(
        _merge_kernel,
        grid=(NT,),
        in_specs=in_specs,
        out_specs=pl.BlockSpec((TM, D), lambda i: (i, 0)),
        out_shape=jax.ShapeDtypeStruct((NA, D), F32),
        scratch_shapes=[pltpu.VMEM((TM + 2 * HALO_Y, CONV_DIM), F32)],
        compiler_params=pltpu.CompilerParams(dimension_semantics=("arbitrary",),
                                             vmem_limit_bytes=VMEM_LIMIT),
        name="merge",
    )(xall, y, y, y, gates, om, og, mod_l, cw, cb, lng, lnb, wco, wmo, wgo, wout)


FF_CHUNK = D_FF // 2


def _ffn_kernel(xp_ref, xc_ref, xn_ref, mod_ref, gn_ref, wup_ref, dw_ref, db_ref, wdown_ref,
                o_ref, ua_ref, ug_ref):
    i = pl.program_id(0)
    left_ok = i >= 2
    right_ok = jnp.logical_and(i >= 1, i < NT - 1)
    xc = xc_ref[...]
    xh = jnp.concatenate([xp_ref[...], xc, xn_ref[...]], axis=0)
    shift = mod_ref[0, 3:4, :]
    scale = mod_ref[0, 4:5, :]
    h = xh * lax.rsqrt(jnp.mean(xh * xh, axis=-1, keepdims=True) + EPS) * gn_ref[...]
    hb = (h * (1.0 + scale) + shift).astype(BF)

    rows = TM + 2 * HALO_X
    row_id = lax.broadcasted_iota(jnp.int32, (rows, 1), 0)
    keep = jnp.logical_and(jnp.logical_or(row_id >= HALO_X, left_ok),
                           jnp.logical_or(row_id < HALO_X + TM, right_ok))

    def conv3(u_ref, col0, width):
        w = dw_ref[:, col0:col0 + width]
        return (u_ref[HALO_X - 1:HALO_X - 1 + TM, :] * w[0:1]
                + u_ref[HALO_X:HALO_X + TM, :] * w[1:2]
                + u_ref[HALO_X + 1:HALO_X + 1 + TM, :] * w[2:3]
                + db_ref[:, col0:col0 + width])

    acc = jnp.zeros((TM, D), F32)
    for c in range(D_FF // FF_CHUNK):
        a0 = c * FF_CHUNK
        g0 = D_FF + c * FF_CHUNK
        ua_ref[...] = jnp.where(keep, _dot(hb, wup_ref[:, a0:a0 + FF_CHUNK]), 0.0)
        ug_ref[...] = jnp.where(keep, _dot(hb, wup_ref[:, g0:g0 + FF_CHUNK]), 0.0)
        a = conv3(ua_ref, a0, FF_CHUNK)
        g = conv3(ug_ref, g0, FF_CHUNK)
        act = (g * _sigmoid(g) * a).astype(BF)
        acc = acc + _dot(act, wdown_ref[a0:a0 + FF_CHUNK, :])
    o_ref[...] = xc + mod_ref[0, 5:6, :] * acc


def _ffn_call(x1, mod_l, gn, wup, dw, db, wdown):
    nx = NA // HALO_X
    per = TM // HALO_X
    in_specs = [
        pl.BlockSpec((HALO_X, D), lambda i: (jnp.maximum(i * per - 1, 0), 0)),
        pl.BlockSpec((TM, D), lambda i: (i, 0)),
        pl.BlockSpec((HALO_X, D), lambda i: (jnp.minimum((i + 1) * per, nx - 1), 0)),
        pl.BlockSpec((1, N_MOD, D), lambda i: (jnp.where(i == 0, 1, 0), 0, 0)),
        _const_spec((1, D)),
        _const_spec((D, 2 * D_FF)),
        _const_spec((3, 2 * D_FF)),
        _const_spec((1, 2 * D_FF)),
        _const_spec((D_FF, D)),
    ]
    return pl.pallas_call(
        _ffn_kernel,
        grid=(NT,),
        in_specs=in_specs,
        out_specs=pl.BlockSpec((TM, D), lambda i: (i, 0)),
        out_shape=jax.ShapeDtypeStruct((NA, D), F32),
        scratch_shapes=[pltpu.VMEM((TM + 2 * HALO_X, FF_CHUNK), F32),
                        pltpu.VMEM((TM + 2 * HALO_X, FF_CHUNK), F32)],
        compiler_params=pltpu.CompilerParams(dimension_semantics=("arbitrary",),
                                             vmem_limit_bytes=VMEM_LIMIT),
        name="ffn",
    )(x1, x1, x1, mod_l, gn, wup, dw, db, wdown)


def _rope_tables(rot_dim):
    n_freq = rot_dim // 4
    inv = 1.0 / (ROPE_THETA ** (jnp.arange(n_freq, dtype=F32) / n_freq))
    pos = jnp.arange(SEQ, dtype=jnp.int32)
    ang_r = (pos // GRID_W).astype(F32)[None, :] * inv[:, None]
    ang_c = (pos % GRID_W).astype(F32)[None, :] * inv[:, None]
    ang = jnp.concatenate([ang_r, ang_r, ang_c, ang_c], axis=0)
    cos = jnp.concatenate([jnp.ones((rot_dim, CTX), F32), jnp.cos(ang)], axis=1)
    sin = jnp.concatenate([jnp.zeros((rot_dim, CTX), F32), jnp.sin(ang)], axis=1)
    return cos, sin


def _pad_col(g, n):
    return jnp.pad(g, (0, n - g.shape[0])).reshape(n, 1)


def kernel(x, c, ctx, c_ctx, w_mod, b_mod, g_norm1, g_norm2, w_in, b_gate, conv_dw_w, conv_dw_b, conv_ln_g, conv_ln_b, w_conv_out, g_q_a, w_q_b, g_kv_a, w_kv_b, g_mla_q, g_mla_k, w_mla_o, g_gqa_q, g_gqa_k, w_gqa_o, w_out, w_up, ffn_dw_w, ffn_dw_b, w_down):
    assert x.shape == (1, SEQ, D) and ctx.shape == (1, CTX, D)
    xall = jnp.concatenate([ctx[0], x[0]], axis=0)
    cond_t = jnp.stack([c[0], c_ctx], axis=1)
    mod = _mod_call(cond_t, w_mod, b_mod).reshape(DEPTH, 2, N_MOD, D)
    cosm, sinm = _rope_tables(MLA_ROPE)
    cosg, sing = _rope_tables(GQA_HEAD_DIM)

    o_qa = 2 * CONV_DIM
    o_kva = o_qa + MLA_Q_RANK
    o_kr = o_kva + MLA_KV_RANK
    o_gq = o_kr + MLA_ROPE
    o_gk = o_gq + GQA_HEADS * GQA_HEAD_DIM
    o_gv = o_gk + GQA_KV_HEADS * GQA_HEAD_DIM
    o_gate = o_gv + GQA_KV_HEADS * GQA_HEAD_DIM

    for l in range(DEPTH):
        wi = w_in[l]
        wmain = jnp.concatenate([wi[:, :o_qa], wi[:, o_gate:], wi[:, o_qa:o_kr]], axis=1).astype(BF)
        wfm = jnp.concatenate([wi[:, o_gq:o_gate], wi[:, o_kr:o_gq]], axis=1).T.astype(BF)
        wqb = jnp.pad(w_q_b[l].reshape(MLA_Q_RANK, MLA_HEADS, MLA_QK),
                      ((0, 0), (0, 0), (0, HEAD_PAD - MLA_QK)))
        wqb = wqb.reshape(MLA_Q_RANK, MLA_HEADS * HEAD_PAD).T.astype(BF)
        wkvb = w_kv_b[l].T.astype(BF)
        mod_l = mod[l]

        y, gates, qtm, km, vtm, knm, qtg, kg, vtg, kng = _proj_call(
            xall, mod_l, g_norm1[l].reshape(1, D), wmain, wfm, wqb, wkvb, b_gate[l].reshape(1, 3 * D),
            g_q_a[l].reshape(1, -1), g_kv_a[l].reshape(1, -1),
            _pad_col(g_mla_q[l], HEAD_PAD), _pad_col(g_mla_k[l], HEAD_PAD),
            g_gqa_q[l].reshape(-1, 1), g_gqa_k[l].reshape(-1, 1), cosm, sinm, cosg, sing)
        om = _attn_call(qtm, km, vtm, knm.reshape(MLA_HEADS, NA // 128, 128),
                        k_per_head=True, v_per_head=True, name="attn_mla")
        og = _attn_call(qtg, kg.reshape(1, NA, HEAD_PAD), vtg, kng.reshape(GQA_KV_HEADS, NA // 128, 128),
                        k_per_head=False, v_per_head=False, name="attn_gqa")
        x1 = _merge_call(xall, y, gates, om, og, mod_l, conv_dw_w[l], conv_dw_b[l].reshape(1, -1),
                         conv_ln_g[l].reshape(1, -1), conv_ln_b[l].reshape(1, -1),
                         w_conv_out[l].astype(BF), w_mla_o[l].astype(BF), w_gqa_o[l].astype(BF),
                         w_out[l].astype(BF))
        xall = _ffn_call(x1, mod_l, g_norm2[l].reshape(1, D), w_up[l].astype(BF), ffn_dw_w[l],
                         ffn_dw_b[l].reshape(1, -1), w_down[l].astype(BF))
    return xall[CTX:][None]
```

```python
from functools import partial

import jax
import jax.numpy as jnp
from jax import lax
from jax.experimental import pallas as pl
from jax.experimental.pallas import tpu as pltpu

D = 1024
SEQ = 16384
DEPTH = 4
GRID_W = 64
CTX = 256
NA = CTX + SEQ
ROPE_THETA = 10000.0
EPS = 1e-6
N_MOD = 6

CONV_DIM = 512
CONV_WIDTH = 31
MLA_HEADS = 8
MLA_Q_RANK = 384
MLA_KV_RANK = 256
MLA_NOPE = 64
MLA_ROPE = 32
MLA_V = 64
MLA_QK = MLA_NOPE + MLA_ROPE
GQA_HEADS = 8
GQA_KV_HEADS = 2
GQA_HEAD_DIM = 64
GQA_GROUP = GQA_HEADS // GQA_KV_HEADS
D_FF = 2816
LOG2E = 1.4426950408889634
MLA_SCALE = MLA_QK ** -0.5 * LOG2E
GQA_SCALE = GQA_HEAD_DIM ** -0.5 * LOG2E

HEAD_PAD = 128
TM = 256
NT = NA // TM
KC = 256
MQ = 256
DEN_FLOOR = 2.0 ** -60
HALO_Y = 16
HALO_X = 8
SUBLANES = 8
CONV_SPAN = TM + (HALO_Y + CONV_WIDTH // 2) // SUBLANES * SUBLANES
VMEM_LIMIT = 56 * 1024 * 1024

BF = jnp.bfloat16
F32 = jnp.float32

C_GLU = 0
C_GATE = 2 * CONV_DIM
C_QA = C_GATE + 3 * D
C_KVA = C_QA + MLA_Q_RANK
W_MAIN = C_KVA + MLA_KV_RANK
R_GQ = 0
R_GK = GQA_HEADS * GQA_HEAD_DIM
R_GV = R_GK + GQA_KV_HEADS * GQA_HEAD_DIM
R_KR = R_GV + GQA_KV_HEADS * GQA_HEAD_DIM
W_FM = R_KR + MLA_ROPE


def _const_spec(shape):
    nd = len(shape)
    return pl.BlockSpec(shape, lambda *_: (0,) * nd, pipeline_mode=pl.Buffered(1))


def _dot(a, b):
    return jnp.dot(a, b, preferred_element_type=F32)


def _dot_nt(a, b):
    return lax.dot_general(a, b, (((1,), (1,)), ((), ())), preferred_element_type=F32)


def _sigmoid(x):
    return 1.0 / (1.0 + jnp.exp(-x))


def _sq_norm_as_stored(kt):
    kq = kt.astype(BF).astype(F32)
    return jnp.sum(kq * kq, axis=0, keepdims=True)


def _rope_rows(x, cos, sin):
    q = x.shape[0] // 4
    x1, x2, x3, x4 = x[0:q], x[q:2 * q], x[2 * q:3 * q], x[3 * q:4 * q]
    rot = jnp.concatenate([-x2, x1, -x4, x3], axis=0)
    return x * cos + rot * sin


MOD_TN = 1536


def _mod_kernel(cond_ref, w_ref, b_ref, o_ref):
    c = cond_ref[...]
    s = c * _sigmoid(c)
    w = w_ref[0]
    b = b_ref[0]
    for j in range(2):
        o_ref[0, j:j + 1, :] = jnp.sum(s[:, j:j + 1] * w, axis=0, keepdims=True) + b


def _mod_call(cond_t, w_mod, b_mod):
    nl = w_mod.shape[0]
    ncol = N_MOD * D
    return pl.pallas_call(
        _mod_kernel,
        grid=(nl, ncol // MOD_TN),
        in_specs=[
            pl.BlockSpec((D, 2), lambda l, j: (0, 0)),
            pl.BlockSpec((1, D, MOD_TN), lambda l, j: (l, 0, j)),
            pl.BlockSpec((1, 1, MOD_TN), lambda l, j: (l, 0, j)),
        ],
        out_specs=pl.BlockSpec((1, 2, MOD_TN), lambda l, j: (l, 0, j)),
        out_shape=jax.ShapeDtypeStruct((nl, 2, ncol), F32),
        compiler_params=pltpu.CompilerParams(vmem_limit_bytes=VMEM_LIMIT),
        name="adaln_mod",
    )(cond_t, w_mod, b_mod.reshape(nl, 1, ncol))


def _proj_kernel(x_ref, mod_ref, gn_ref, wmain_ref, wfm_ref, wqb_ref, wkvb_ref, bgate_ref,
                 gqa_ref, gkva_ref, gmq_ref, gmk_ref, ggq_ref, ggk_ref,
                 cosm_ref, sinm_ref, cosg_ref, sing_ref,
                 y_ref, gates_ref, qtm_ref, km_ref, vtm_ref, knm_ref, qtg_ref, kg_ref, vtg_ref, kng_ref):
    x = x_ref[...]
    shift = mod_ref[0, 0:1, :]
    scale = mod_ref[0, 1:2, :]
    h = x * lax.rsqrt(jnp.mean(x * x, axis=-1, keepdims=True) + EPS) * gn_ref[...]
    hb = (h * (1.0 + scale) + shift).astype(BF)

    za = _dot(hb, wmain_ref[:, C_GLU:C_GLU + CONV_DIM])
    zg = _dot(hb, wmain_ref[:, C_GLU + CONV_DIM:C_GATE])
    y_ref[...] = za * _sigmoid(zg)
    zgate = _dot(hb, wmain_ref[:, C_GATE:C_QA])
    gates_ref[...] = _sigmoid(zgate + bgate_ref[...])

    zqa = _dot(hb, wmain_ref[:, C_QA:C_KVA])
    qa = (zqa * lax.rsqrt(jnp.mean(zqa * zqa, axis=-1, keepdims=True) + EPS) * gqa_ref[...]).astype(BF)
    zkva = _dot(hb, wmain_ref[:, C_KVA:W_MAIN])
    kva = (zkva * lax.rsqrt(jnp.mean(zkva * zkva, axis=-1, keepdims=True) + EPS) * gkva_ref[...]).astype(BF)
    qt = _dot_nt(wqb_ref[...], qa)
    kvt = _dot_nt(wkvb_ref[...], kva)
    fm = _dot_nt(wfm_ref[...], hb)

    cosm, sinm = cosm_ref[...], sinm_ref[...]
    cosg, sing = cosg_ref[...], sing_ref[...]
    gmq, gmk = gmq_ref[...], gmk_ref[...]
    ggq, ggk = ggq_ref[...], ggk_ref[...]
    kr = fm[R_KR:R_KR + MLA_ROPE]
    kr_ss = jnp.sum(kr * kr, axis=0, keepdims=True)
    zpad = jnp.zeros((HEAD_PAD - MLA_QK, TM), F32)

    for hd in range(MLA_HEADS):
        q = qt[hd * HEAD_PAD:(hd + 1) * HEAD_PAD]
        r = lax.rsqrt(jnp.sum(q * q, axis=0, keepdims=True) * (1.0 / MLA_QK) + EPS)
        qn = q * r * gmq
        qr = _rope_rows(qn[MLA_NOPE:MLA_QK], cosm, sinm)
        qfull = jnp.concatenate([qn[:MLA_NOPE], qr, zpad], axis=0) * MLA_SCALE
        qtm_ref[hd] = qfull.astype(BF)

        base = hd * (MLA_NOPE + MLA_V)
        kn = kvt[base:base + MLA_NOPE]
        vtm_ref[0, hd * MLA_V:(hd + 1) * MLA_V, :] = kvt[base + MLA_NOPE:base + MLA_NOPE + MLA_V].astype(BF)
        rk = lax.rsqrt((jnp.sum(kn * kn, axis=0, keepdims=True) + kr_ss) * (1.0 / MLA_QK) + EPS)
        knn = kn * rk * gmk[:MLA_NOPE]
        krn = _rope_rows(kr * rk * gmk[MLA_NOPE:MLA_QK], cosm, sinm)
        kfull = jnp.concatenate([knn, krn, zpad], axis=0)
        km_ref[hd] = kfull.T.astype(BF)
        knm_ref[hd:hd + 1, :] = _sq_norm_as_stored(kfull)

    zhalf = jnp.zeros((GQA_HEAD_DIM, TM), F32)
    for hd in range(GQA_HEADS):
        q = fm[R_GQ + hd * GQA_HEAD_DIM:R_GQ + (hd + 1) * GQA_HEAD_DIM]
        r = lax.rsqrt(jnp.mean(q * q, axis=0, keepdims=True) + EPS)
        qn = _rope_rows(q * r * ggq, cosg, sing) * GQA_SCALE
        parts = [qn, zhalf] if hd // GQA_GROUP == 0 else [zhalf, qn]
        qtg_ref[hd] = jnp.concatenate(parts, axis=0).astype(BF)
    ks = []
    for g in range(GQA_KV_HEADS):
        k = fm[R_GK + g * GQA_HEAD_DIM:R_GK + (g + 1) * GQA_HEAD_DIM]
        r = lax.rsqrt(jnp.mean(k * k, axis=0, keepdims=True) + EPS)
        ks.append(_rope_rows(k * r * ggk, cosg, sing))
        kng_ref[g:g + 1, :] = _sq_norm_as_stored(ks[g])
    kg_ref[...] = jnp.concatenate(ks, axis=0).T.astype(BF)
    vtg_ref[0] = fm[R_GV:R_KR].astype(BF)


def _proj_call(xall, mod_l, gn, wmain, wfm, wqb, wkvb, bgate, gqa, gkva, gmq, gmk, ggq, ggk,
               cosm, sinm, cosg, sing):
    def stream(i):
        return jnp.where(i == 0, 1, 0)

    in_specs = [
        pl.BlockSpec((TM, D), lambda i: (i, 0)),
        pl.BlockSpec((1, N_MOD, D), lambda i: (stream(i), 0, 0)),
        _const_spec((1, D)),
        _const_spec((D, W_MAIN)),
        _const_spec((W_FM, D)),
        _const_spec((MLA_HEADS * HEAD_PAD, MLA_Q_RANK)),
        _const_spec((MLA_HEADS * (MLA_NOPE + MLA_V), MLA_KV_RANK)),
        _const_spec((1, 3 * D)),
        _const_spec((1, MLA_Q_RANK)),
        _const_spec((1, MLA_KV_RANK)),
        _const_spec((HEAD_PAD, 1)),
        _const_spec((HEAD_PAD, 1)),
        _const_spec((GQA_HEAD_DIM, 1)),
        _const_spec((GQA_HEAD_DIM, 1)),
        pl.BlockSpec((MLA_ROPE, TM), lambda i: (0, i)),
        pl.BlockSpec((MLA_ROPE, TM), lambda i: (0, i)),
        pl.BlockSpec((GQA_HEAD_DIM, TM), lambda i: (0, i)),
        pl.BlockSpec((GQA_HEAD_DIM, TM), lambda i: (0, i)),
    ]
    out_specs = [
        pl.BlockSpec((TM, CONV_DIM), lambda i: (i, 0)),
        pl.BlockSpec((TM, 3 * D), lambda i: (i, 0)),
        pl.BlockSpec((MLA_HEADS, HEAD_PAD, TM), lambda i: (0, 0, i)),
        pl.BlockSpec((MLA_HEADS, TM, HEAD_PAD), lambda i: (0, i, 0)),
        pl.BlockSpec((1, MLA_HEADS * MLA_V, KC), lambda i: (i, 0, 0)),
        pl.BlockSpec((MLA_HEADS, TM), lambda i: (0, i)),
        pl.BlockSpec((GQA_HEADS, HEAD_PAD, TM), lambda i: (0, 0, i)),
        pl.BlockSpec((TM, HEAD_PAD), lambda i: (i, 0)),
        pl.BlockSpec((1, GQA_KV_HEADS * GQA_HEAD_DIM, KC), lambda i: (i, 0, 0)),
        pl.BlockSpec((GQA_KV_HEADS, TM), lambda i: (0, i)),
    ]
    out_shape = [
        jax.ShapeDtypeStruct((NA, CONV_DIM), F32),
        jax.ShapeDtypeStruct((NA, 3 * D), F32),
        jax.ShapeDtypeStruct((MLA_HEADS, HEAD_PAD, NA), BF),
        jax.ShapeDtypeStruct((MLA_HEADS, NA, HEAD_PAD), BF),
        jax.ShapeDtypeStruct((NT, MLA_HEADS * MLA_V, KC), BF),
        jax.ShapeDtypeStruct((MLA_HEADS, NA), F32),
        jax.ShapeDtypeStruct((GQA_HEADS, HEAD_PAD, NA), BF),
        jax.ShapeDtypeStruct((NA, HEAD_PAD), BF),
        jax.ShapeDtypeStruct((NT, GQA_KV_HEADS * GQA_HEAD_DIM, KC), BF),
        jax.ShapeDtypeStruct((GQA_KV_HEADS, NA), F32),
    ]
    return pl.pallas_call(
        _proj_kernel,
        grid=(NT,),
        in_specs=in_specs,
        out_specs=out_specs,
        out_shape=out_shape,
        compiler_params=pltpu.CompilerParams(dimension_semantics=("arbitrary",),
                                             vmem_limit_bytes=VMEM_LIMIT),
        name="proj",
    )(xall, mod_l, gn, wmain, wfm, wqb, wkvb, bgate, gqa, gkva, gmq, gmk, ggq, ggk,
      cosm, sinm, cosg, sing)


def _attn_kernel(q_ref, k_ref, vt_ref, kn_ref, o_ref, *, k_per_head, v_per_head):
    j = pl.program_id(1)

    def k_head(hh):
        return hh if k_per_head else 0

    def v_rows(hh):
        v0 = hh * MLA_V if v_per_head else 0
        return slice(v0, v0 + MLA_V)

    def write(outs):
        o_ref[...] = jnp.concatenate(outs, axis=0).T.astype(o_ref.dtype)

    def scores(hh, c):
        start = pl.multiple_of(c * KC, KC)
        return _dot(k_ref[k_head(hh), pl.ds(start, KC), :], q_ref[hh])

    def pv(hh, c, p):
        return _dot(vt_ref[c, v_rows(hh), :], p.astype(BF))

    def online(hh, n_chunks):
        s = scores(hh, 0)
        m = jnp.max(s, axis=0, keepdims=True)
        p = jnp.exp2(s - m)
        carry = (m, jnp.sum(p, axis=0, keepdims=True), pv(hh, 0, p))

        def body(c, carry):
            m, den, acc = carry
            s = scores(hh, c)
            m_new = jnp.maximum(m, jnp.max(s, axis=0, keepdims=True))
            alpha = jnp.exp2(m - m_new)
            p = jnp.exp2(s - m_new)
            return m_new, alpha * den + jnp.sum(p, axis=0, keepdims=True), alpha * acc + pv(hh, c, p)

        _, den, acc = lax.fori_loop(1, n_chunks, body, carry)
        return acc / den

    @pl.when(j == 0)
    def _():
        write([online(hh, 1) for hh in range(2)])

    @pl.when(j > 0)
    def _():
        shift = []
        for hh in range(2):
            qf = q_ref[hh].astype(F32)
            qn2 = jnp.sum(qf * qf, axis=0, keepdims=True)
            kn2 = jnp.max(jnp.max(kn_ref[k_head(hh)], axis=0, keepdims=True), axis=1, keepdims=True)
            shift.append(jnp.sqrt(qn2 * kn2))
        acc = [jnp.zeros((MLA_V, MQ), F32)] * 2
        den = [jnp.zeros((1, MQ), F32)] * 2
        p_prev = [None, None]
        for c in range(NT + 1):
            p_new = [None, None]
            if c < NT:
                for hh in range(2):
                    s = _dot(k_ref[k_head(hh), c * KC:(c + 1) * KC, :], q_ref[hh])
                    e = jnp.exp2(s - shift[hh])
                    den[hh] = den[hh] + jnp.sum(e, axis=0, keepdims=True)
                    p_new[hh] = e.astype(BF)
            if c >= 1:
                for hh in range(2):
                    acc[hh] = acc[hh] + _dot(vt_ref[c - 1, v_rows(hh), :], p_prev[hh])
            p_prev = p_new
        write([a / d for a, d in zip(acc, den)])
        den_min = jnp.min(jnp.minimum(den[0], den[1]))

        @pl.when(jnp.logical_not(den_min >= DEN_FLOOR))
        def _():
            write([online(hh, NT) for hh in range(2)])


def _attn_call(qt, k, vt, kn, *, k_per_head, v_per_head, name):
    if k_per_head:
        k_spec = pl.BlockSpec((2, NA, HEAD_PAD), lambda p, j: (p, 0, 0))
        kn_spec = pl.BlockSpec((2, NA // 128, 128), lambda p, j: (p, 0, 0))
    else:
        k_spec = pl.BlockSpec((1, NA, HEAD_PAD), lambda p, j: (0, 0, 0))
        kn_spec = pl.BlockSpec((1, NA // 128, 128), lambda p, j: (p // (GQA_GROUP // 2), 0, 0))
    if v_per_head:
        v_spec = pl.BlockSpec((NT, 2 * MLA_V, KC), lambda p, j: (0, p, 0))
    else:
        v_spec = pl.BlockSpec((NT, GQA_HEAD_DIM, KC), lambda p, j: (0, p // (GQA_GROUP // 2), 0))
    return pl.pallas_call(
        partial(_attn_kernel, k_per_head=k_per_head, v_per_head=v_per_head),
        grid=(4, NA // MQ),
        in_specs=[pl.BlockSpec((2, HEAD_PAD, MQ), lambda p, j: (p, 0, j)), k_spec, v_spec, kn_spec],
        out_specs=pl.BlockSpec((MQ, 2 * MLA_V), lambda p, j: (j, p)),
        out_shape=jax.ShapeDtypeStruct((NA, 8 * MLA_V), BF),
        compiler_params=pltpu.CompilerParams(dimension_semantics=("arbitrary", "arbitrary"),
                                             vmem_limit_bytes=VMEM_LIMIT),
        name=name,
    )(qt, k, vt, kn)


def _merge_kernel(x_ref, yp_ref, yc_ref, yn_ref, gates_ref, om_ref, og_ref, mod_ref,
                  cw_ref, cb_ref, lng_ref, lnb_ref, wco_ref, wmo_ref, wgo_ref, wout_ref,
                  o_ref, buf_ref, sh_ref):
    i = pl.program_id(0)
    left_ok = i >= 2
    right_ok = jnp.logical_and(i >= 1, i < NT - 1)
    buf_ref[0:HALO_Y, :] = jnp.where(left_ok, yp_ref[...], 0.0)
    buf_ref[HALO_Y:HALO_Y + TM, :] = yc_ref[...]
    buf_ref[HALO_Y + TM:HALO_Y + TM + HALO_Y, :] = jnp.where(right_ok, yn_ref[...], 0.0)

    for r in range(1, SUBLANES):
        sh_ref[r - 1] = buf_ref[r:r + CONV_SPAN, :]
    off = HALO_Y - CONV_WIDTH // 2
    acc = jnp.zeros((TM, CONV_DIM), F32) + cb_ref[...]
    for k in range(CONV_WIDTH):
        a, r = divmod(off + k, SUBLANES)
        src = buf_ref if r == 0 else sh_ref.at[r - 1]
        acc = acc + src[a * SUBLANES:a * SUBLANES + TM, :] * cw_ref[k:k + 1, :]
    mu = jnp.mean(acc, axis=-1, keepdims=True)
    xc = acc - mu
    var = jnp.mean(xc * xc, axis=-1, keepdims=True)
    ln = xc * lax.rsqrt(var + EPS) * lng_ref[...] + lnb_ref[...]
    act = (ln * _sigmoid(ln)).astype(BF)
    br_conv = _dot(act, wco_ref[...])
    br_mla = _dot(om_ref[...], wmo_ref[...])
    br_gqa = _dot(og_ref[...], wgo_ref[...])
    merged = (gates_ref[:, 0:D] * br_conv + gates_ref[:, D:2 * D] * br_mla
              + gates_ref[:, 2 * D:3 * D] * br_gqa)
    res = _dot(merged.astype(BF), wout_ref[...])
    o_ref[...] = x_ref[...] + mod_ref[0, 2:3, :] * res


def _merge_call(xall, y, gates, om, og, mod_l, cw, cb, lng, lnb, wco, wmo, wgo, wout):
    ny = NA // HALO_Y
    per = TM // HALO_Y
    in_specs = [
        pl.BlockSpec((TM, D), lambda i: (i, 0)),
        pl.BlockSpec((HALO_Y, CONV_DIM), lambda i: (jnp.maximum(i * per - 1, 0), 0)),
        pl.BlockSpec((TM, CONV_DIM), lambda i: (i, 0)),
        pl.BlockSpec((HALO_Y, CONV_DIM), lambda i: (jnp.minimum((i + 1) * per, ny - 1), 0)),
        pl.BlockSpec((TM, 3 * D), lambda i: (i, 0)),
        pl.BlockSpec((TM, 8 * MLA_V), lambda i: (i, 0)),
        pl.BlockSpec((TM, 8 * MLA_V), lambda i: (i, 0)),
        pl.BlockSpec((1, N_MOD, D), lambda i: (jnp.where(i == 0, 1, 0), 0, 0)),
        _const_spec((CONV_WIDTH, CONV_DIM)),
        _const_spec((1, CONV_DIM)),
        _const_spec((1, CONV_DIM)),
        _const_spec((1, CONV_DIM)),
        _const_spec((CONV_DIM, D)),
        _const_spec((8 * MLA_V, D)),
        _const_spec((8 * GQA_HEAD_DIM, D)),
        _const_spec((D, D)),
    ]
    return pl.pallas_call(
        _merge_kernel,
        grid=(NT,),
        in_specs=in_specs,
        out_specs=pl.BlockSpec((TM, D), lambda i: (i, 0)),
        out_shape=jax.ShapeDtypeStruct((NA, D), F32),
        scratch_shapes=[pltpu.VMEM((TM + 2 * HALO_Y, CONV_DIM), F32),
                        pltpu.VMEM((SUBLANES - 1, CONV_SPAN, CONV_DIM), F32)],
        compiler_params=pltpu.CompilerParams(dimension_semantics=("arbitrary",),
                                             vmem_limit_bytes=VMEM_LIMIT),
        name="merge",
    )(xall, y, y, y, gates, om, og, mod_l, cw, cb, lng, lnb, wco, wmo, wgo, wout)


FF_CHUNK = D_FF


def _ffn_kernel(xp_ref, xc_ref, xn_ref, mod_ref, gn_ref, wup_ref, dw_ref, db_ref, wdown_ref,
                o_ref, ua_ref, ug_ref):
    i = pl.program_id(0)
    left_ok = i >= 2
    right_ok = jnp.logical_and(i >= 1, i < NT - 1)
    xc = xc_ref[...]
    xh = jnp.concatenate([xp_ref[...], xc, xn_ref[...]], axis=0)
    shift = mod_ref[0, 3:4, :]
    scale = mod_ref[0, 4:5, :]
    h = xh * lax.rsqrt(jnp.mean(xh * xh, axis=-1, keepdims=True) + EPS) * gn_ref[...]
    hb = (h * (1.0 + scale) + shift).astype(BF)

    rows = TM + 2 * HALO_X
    row_id = lax.broadcasted_iota(jnp.int32, (rows, 1), 0)
    keep = jnp.logical_and(jnp.logical_or(row_id >= HALO_X, left_ok),
                           jnp.logical_or(row_id < HALO_X + TM, right_ok))

    def conv3(u_ref, col0, width):
        w = dw_ref[:, col0:col0 + width]
        return (u_ref[HALO_X - 1:HALO_X - 1 + TM, :] * w[0:1]
                + u_ref[HALO_X:HALO_X + TM, :] * w[1:2]
                + u_ref[HALO_X + 1:HALO_X + 1 + TM, :] * w[2:3]
                + db_ref[:, col0:col0 + width])

    acc = jnp.zeros((TM, D), F32)
    for c in range(D_FF // FF_CHUNK):
        a0 = c * FF_CHUNK
        g0 = D_FF + c * FF_CHUNK
        ua_ref[...] = jnp.where(keep, _dot(hb, wup_ref[:, a0:a0 + FF_CHUNK]), 0.0)
        ug_ref[...] = jnp.where(keep, _dot(hb, wup_ref[:, g0:g0 + FF_CHUNK]), 0.0)
        a = conv3(ua_ref, a0, FF_CHUNK)
        g = conv3(ug_ref, g0, FF_CHUNK)
        act = (g * _sigmoid(g) * a).astype(BF)
        acc = acc + _dot(act, wdown_ref[a0:a0 + FF_CHUNK, :])
    o_ref[...] = xc + mod_ref[0, 5:6, :] * acc


def _ffn_call(x1, mod_l, gn, wup, dw, db, wdown):
    nx = NA // HALO_X
    per = TM // HALO_X
    in_specs = [
        pl.BlockSpec((HALO_X, D), lambda i: (jnp.maximum(i * per - 1, 0), 0)),
        pl.BlockSpec((TM, D), lambda i: (i, 0)),
        pl.BlockSpec((HALO_X, D), lambda i: (jnp.minimum((i + 1) * per, nx - 1), 0)),
        pl.BlockSpec((1, N_MOD, D), lambda i: (jnp.where(i == 0, 1, 0), 0, 0)),
        _const_spec((1, D)),
        _const_spec((D, 2 * D_FF)),
        _const_spec((3, 2 * D_FF)),
        _const_spec((1, 2 * D_FF)),
        _const_spec((D_FF, D)),
    ]
    return pl.pallas_call(
        _ffn_kernel,
        grid=(NT,),
        in_specs=in_specs,
        out_specs=pl.BlockSpec((TM, D), lambda i: (i, 0)),
        out_shape=jax.ShapeDtypeStruct((NA, D), F32),
        scratch_shapes=[pltpu.VMEM((TM + 2 * HALO_X, FF_CHUNK), F32),
                        pltpu.VMEM((TM + 2 * HALO_X, FF_CHUNK), F32)],
        compiler_params=pltpu.CompilerParams(dimension_semantics=("arbitrary",),
                                             vmem_limit_bytes=VMEM_LIMIT),
        name="ffn",
    )(x1, x1, x1, mod_l, gn, wup, dw, db, wdown)


def _rope_tables(rot_dim):
    n_freq = rot_dim // 4
    inv = 1.0 / (ROPE_THETA ** (jnp.arange(n_freq, dtype=F32) / n_freq))
    pos = jnp.arange(SEQ, dtype=jnp.int32)
    ang_r = (pos // GRID_W).astype(F32)[None, :] * inv[:, None]
    ang_c = (pos % GRID_W).astype(F32)[None, :] * inv[:, None]
    ang = jnp.concatenate([ang_r, ang_r, ang_c, ang_c], axis=0)
    cos = jnp.concatenate([jnp.ones((rot_dim, CTX), F32), jnp.cos(ang)], axis=1)
    sin = jnp.concatenate([jnp.zeros((rot_dim, CTX), F32), jnp.sin(ang)], axis=1)
    return cos, sin


def _pad_col(g, n):
    return jnp.pad(g, (0, n - g.shape[0])).reshape(n, 1)


def kernel(x, c, ctx, c_ctx, w_mod, b_mod, g_norm1, g_norm2, w_in, b_gate, conv_dw_w, conv_dw_b, conv_ln_g, conv_ln_b, w_conv_out, g_q_a, w_q_b, g_kv_a, w_kv_b, g_mla_q, g_mla_k, w_mla_o, g_gqa_q, g_gqa_k, w_gqa_o, w_out, w_up, ffn_dw_w, ffn_dw_b, w_down):
    assert x.shape == (1, SEQ, D) and ctx.shape == (1, CTX, D)
    xall = jnp.concatenate([ctx[0], x[0]], axis=0)
    cond_t = jnp.stack([c[0], c_ctx], axis=1)
    mod = _mod_call(cond_t, w_mod, b_mod).reshape(DEPTH, 2, N_MOD, D)
    cosm, sinm = _rope_tables(MLA_ROPE)
    cosg, sing = _rope_tables(GQA_HEAD_DIM)

    o_qa = 2 * CONV_DIM
    o_kva = o_qa + MLA_Q_RANK
    o_kr = o_kva + MLA_KV_RANK
    o_gq = o_kr + MLA_ROPE
    o_gk = o_gq + GQA_HEADS * GQA_HEAD_DIM
    o_gv = o_gk + GQA_KV_HEADS * GQA_HEAD_DIM
    o_gate = o_gv + GQA_KV_HEADS * GQA_HEAD_DIM

    for l in range(DEPTH):
        wi = w_in[l]
        wmain = jnp.concatenate([wi[:, :o_qa], wi[:, o_gate:], wi[:, o_qa:o_kr]], axis=1).astype(BF)
        wfm = jnp.concatenate([wi[:, o_gq:o_gate], wi[:, o_kr:o_gq]], axis=1).T.astype(BF)
        wqb = jnp.pad(w_q_b[l].reshape(MLA_Q_RANK, MLA_HEADS, MLA_QK),
                      ((0, 0), (0, 0), (0, HEAD_PAD - MLA_QK)))
        wqb = wqb.reshape(MLA_Q_RANK, MLA_HEADS * HEAD_PAD).T.astype(BF)
        wkvb = w_kv_b[l].T.astype(BF)
        mod_l = mod[l]

        y, gates, qtm, km, vtm, knm, qtg, kg, vtg, kng = _proj_call(
            xall, mod_l, g_norm1[l].reshape(1, D), wmain, wfm, wqb, wkvb, b_gate[l].reshape(1, 3 * D),
            g_q_a[l].reshape(1, -1), g_kv_a[l].reshape(1, -1),
            _pad_col(g_mla_q[l], HEAD_PAD), _pad_col(g_mla_k[l], HEAD_PAD),
            g_gqa_q[l].reshape(-1, 1), g_gqa_k[l].reshape(-1, 1), cosm, sinm, cosg, sing)
        om = _attn_call(qtm, km, vtm, knm.reshape(MLA_HEADS, NA // 128, 128),
                        k_per_head=True, v_per_head=True, name="attn_mla")
        og = _attn_call(qtg, kg.reshape(1, NA, HEAD_PAD), vtg, kng.reshape(GQA_KV_HEADS, NA // 128, 128),
                        k_per_head=False, v_per_head=False, name="attn_gqa")
        x1 = _merge_call(xall, y, gates, om, og, mod_l, conv_dw_w[l], conv_dw_b[l].reshape(1, -1),
                         conv_ln_g[l].reshape(1, -1), conv_ln_b[l].reshape(1, -1),
                         w_conv_out[l].astype(BF), w_mla_o[l].astype(BF), w_gqa_o[l].astype(BF),
                         w_out[l].astype(BF))
        xall = _ffn_call(x1, mod_l, g_norm2[l].reshape(1, D), w_up[l].astype(BF), ffn_dw_w[l],
                         ffn_dw_b[l].reshape(1, -1), w_down[l].astype(BF))
    return xall[CTX:][None]
```

```python
from functools import partial

import jax
import jax.numpy as jnp
from jax import lax
from jax.experimental import pallas as pl
from jax.experimental.pallas import tpu as pltpu

D = 1024
SEQ = 16384
DEPTH = 4
GRID_W = 64
CTX = 256
NA = CTX + SEQ
ROPE_THETA = 10000.0
EPS = 1e-6
N_MOD = 6

CONV_DIM = 512
CONV_WIDTH = 31
MLA_HEADS = 8
MLA_Q_RANK = 384
MLA_KV_RANK = 256
MLA_NOPE = 64
MLA_ROPE = 32
MLA_V = 64
MLA_QK = MLA_NOPE + MLA_ROPE
GQA_HEADS = 8
GQA_KV_HEADS = 2
GQA_HEAD_DIM = 64
GQA_GROUP = GQA_HEADS // GQA_KV_HEADS
D_FF = 2816
LOG2E = 1.4426950408889634
MLA_SCALE = MLA_QK ** -0.5 * LOG2E
GQA_SCALE = GQA_HEAD_DIM ** -0.5 * LOG2E

HEAD_PAD = 128
TM = 256
NT = NA // TM
KC = 256
MQ = 256
QK_GROUP = 13
DEN_FLOOR = 2.0 ** -60
HALO_Y = 16
HALO_X = 8
SUBLANES = 8
CONV_SPAN = TM + (HALO_Y + CONV_WIDTH // 2) // SUBLANES * SUBLANES
VMEM_LIMIT = 56 * 1024 * 1024

BF = jnp.bfloat16
F32 = jnp.float32

C_GLU = 0
C_GATE = 2 * CONV_DIM
C_QA = C_GATE + 3 * D
C_KVA = C_QA + MLA_Q_RANK
W_MAIN = C_KVA + MLA_KV_RANK
R_GQ = 0
R_GK = GQA_HEADS * GQA_HEAD_DIM
R_GV = R_GK + GQA_KV_HEADS * GQA_HEAD_DIM
R_KR = R_GV + GQA_KV_HEADS * GQA_HEAD_DIM
W_FM = R_KR + MLA_ROPE


def _const_spec(shape):
    nd = len(shape)
    return pl.BlockSpec(shape, lambda *_: (0,) * nd, pipeline_mode=pl.Buffered(1))


def _dot(a, b):
    return jnp.dot(a, b, preferred_element_type=F32)


def _dot_nt(a, b):
    return lax.dot_general(a, b, (((1,), (1,)), ((), ())), preferred_element_type=F32)


def _sigmoid(x):
    return 1.0 / (1.0 + jnp.exp(-x))


def _sq_norm_as_stored(kt):
    kq = kt.astype(BF).astype(F32)
    return jnp.sum(kq * kq, axis=0, keepdims=True)


def _rope_rows(x, cos, sin):
    q = x.shape[0] // 4
    x1, x2, x3, x4 = x[0:q], x[q:2 * q], x[2 * q:3 * q], x[3 * q:4 * q]
    rot = jnp.concatenate([-x2, x1, -x4, x3], axis=0)
    return x * cos + rot * sin


MOD_TN = 1536


def _mod_kernel(cond_ref, w_ref, b_ref, o_ref):
    c = cond_ref[...]
    s = c * _sigmoid(c)
    w = w_ref[0]
    b = b_ref[0]
    for j in range(2):
        o_ref[0, j:j + 1, :] = jnp.sum(s[:, j:j + 1] * w, axis=0, keepdims=True) + b


def _mod_call(cond_t, w_mod, b_mod):
    nl = w_mod.shape[0]
    ncol = N_MOD * D
    return pl.pallas_call(
        _mod_kernel,
        grid=(nl, ncol // MOD_TN),
        in_specs=[
            pl.BlockSpec((D, 2), lambda l, j: (0, 0)),
            pl.BlockSpec((1, D, MOD_TN), lambda l, j: (l, 0, j)),
            pl.BlockSpec((1, 1, MOD_TN), lambda l, j: (l, 0, j)),
        ],
        out_specs=pl.BlockSpec((1, 2, MOD_TN), lambda l, j: (l, 0, j)),
        out_shape=jax.ShapeDtypeStruct((nl, 2, ncol), F32),
        compiler_params=pltpu.CompilerParams(vmem_limit_bytes=VMEM_LIMIT),
        name="adaln_mod",
    )(cond_t, w_mod, b_mod.reshape(nl, 1, ncol))


def _proj_kernel(x_ref, mod_ref, gn_ref, wmain_ref, wfm_ref, wqb_ref, wkvb_ref, bgate_ref,
                 gqa_ref, gkva_ref, gmq_ref, gmk_ref, ggq_ref, ggk_ref,
                 cosm_ref, sinm_ref, cosg_ref, sing_ref,
                 y_ref, gates_ref, qtm_ref, km_ref, vtm_ref, knm_ref, qtg_ref, kg_ref, vtg_ref, kng_ref):
    x = x_ref[...]
    shift = mod_ref[0, 0:1, :]
    scale = mod_ref[0, 1:2, :]
    h = x * lax.rsqrt(jnp.mean(x * x, axis=-1, keepdims=True) + EPS) * gn_ref[...]
    hb = (h * (1.0 + scale) + shift).astype(BF)

    za = _dot(hb, wmain_ref[:, C_GLU:C_GLU + CONV_DIM])
    zg = _dot(hb, wmain_ref[:, C_GLU + CONV_DIM:C_GATE])
    y_ref[...] = za * _sigmoid(zg)
    zgate = _dot(hb, wmain_ref[:, C_GATE:C_QA])
    gates_ref[...] = _sigmoid(zgate + bgate_ref[...])

    zqa = _dot(hb, wmain_ref[:, C_QA:C_KVA])
    qa = (zqa * lax.rsqrt(jnp.mean(zqa * zqa, axis=-1, keepdims=True) + EPS) * gqa_ref[...]).astype(BF)
    zkva = _dot(hb, wmain_ref[:, C_KVA:W_MAIN])
    kva = (zkva * lax.rsqrt(jnp.mean(zkva * zkva, axis=-1, keepdims=True) + EPS) * gkva_ref[...]).astype(BF)
    qt = _dot_nt(wqb_ref[...], qa)
    kvt = _dot_nt(wkvb_ref[...], kva)
    fm = _dot_nt(wfm_ref[...], hb)

    cosm, sinm = cosm_ref[...], sinm_ref[...]
    cosg, sing = cosg_ref[...], sing_ref[...]
    gmq, gmk = gmq_ref[...], gmk_ref[...]
    ggq, ggk = ggq_ref[...], ggk_ref[...]
    kr = fm[R_KR:R_KR + MLA_ROPE]
    kr_ss = jnp.sum(kr * kr, axis=0, keepdims=True)
    zpad = jnp.zeros((HEAD_PAD - MLA_QK, TM), F32)

    for hd in range(MLA_HEADS):
        q = qt[hd * HEAD_PAD:(hd + 1) * HEAD_PAD]
        r = lax.rsqrt(jnp.sum(q * q, axis=0, keepdims=True) * (1.0 / MLA_QK) + EPS)
        qn = q * r * gmq
        qr = _rope_rows(qn[MLA_NOPE:MLA_QK], cosm, sinm)
        qfull = jnp.concatenate([qn[:MLA_NOPE], qr, zpad], axis=0) * MLA_SCALE
        qtm_ref[hd] = qfull.astype(BF)

        base = hd * (MLA_NOPE + MLA_V)
        kn = kvt[base:base + MLA_NOPE]
        vtm_ref[0, hd * MLA_V:(hd + 1) * MLA_V, :] = kvt[base + MLA_NOPE:base + MLA_NOPE + MLA_V].astype(BF)
        rk = lax.rsqrt((jnp.sum(kn * kn, axis=0, keepdims=True) + kr_ss) * (1.0 / MLA_QK) + EPS)
        knn = kn * rk * gmk[:MLA_NOPE]
        krn = _rope_rows(kr * rk * gmk[MLA_NOPE:MLA_QK], cosm, sinm)
        kfull = jnp.concatenate([knn, krn, zpad], axis=0)
        km_ref[hd] = kfull.T.astype(BF)
        knm_ref[hd:hd + 1, :] = _sq_norm_as_stored(kfull)

    zhalf = jnp.zeros((GQA_HEAD_DIM, TM), F32)
    for hd in range(GQA_HEADS):
        q = fm[R_GQ + hd * GQA_HEAD_DIM:R_GQ + (hd + 1) * GQA_HEAD_DIM]
        r = lax.rsqrt(jnp.mean(q * q, axis=0, keepdims=True) + EPS)
        qn = _rope_rows(q * r * ggq, cosg, sing) * GQA_SCALE
        parts = [qn, zhalf] if hd // GQA_GROUP == 0 else [zhalf, qn]
        qtg_ref[hd] = jnp.concatenate(parts, axis=0).astype(BF)
    ks = []
    for g in range(GQA_KV_HEADS):
        k = fm[R_GK + g * GQA_HEAD_DIM:R_GK + (g + 1) * GQA_HEAD_DIM]
        r = lax.rsqrt(jnp.mean(k * k, axis=0, keepdims=True) + EPS)
        ks.append(_rope_rows(k * r * ggk, cosg, sing))
        kng_ref[g:g + 1, :] = _sq_norm_as_stored(ks[g])
    kg_ref[...] = jnp.concatenate(ks, axis=0).T.astype(BF)
    vtg_ref[0] = fm[R_GV:R_KR].astype(BF)


def _proj_call(xall, mod_l, gn, wmain, wfm, wqb, wkvb, bgate, gqa, gkva, gmq, gmk, ggq, ggk,
               cosm, sinm, cosg, sing):
    def stream(i):
        return jnp.where(i == 0, 1, 0)

    in_specs = [
        pl.BlockSpec((TM, D), lambda i: (i, 0)),
        pl.BlockSpec((1, N_MOD, D), lambda i: (stream(i), 0, 0)),
        _const_spec((1, D)),
        _const_spec((D, W_MAIN)),
        _const_spec((W_FM, D)),
        _const_spec((MLA_HEADS * HEAD_PAD, MLA_Q_RANK)),
        _const_spec((MLA_HEADS * (MLA_NOPE + MLA_V), MLA_KV_RANK)),
        _const_spec((1, 3 * D)),
        _const_spec((1, MLA_Q_RANK)),
        _const_spec((1, MLA_KV_RANK)),
        _const_spec((HEAD_PAD, 1)),
        _const_spec((HEAD_PAD, 1)),
        _const_spec((GQA_HEAD_DIM, 1)),
        _const_spec((GQA_HEAD_DIM, 1)),
        pl.BlockSpec((MLA_ROPE, TM), lambda i: (0, i)),
        pl.BlockSpec((MLA_ROPE, TM), lambda i: (0, i)),
        pl.BlockSpec((GQA_HEAD_DIM, TM), lambda i: (0, i)),
        pl.BlockSpec((GQA_HEAD_DIM, TM), lambda i: (0, i)),
    ]
    out_specs = [
        pl.BlockSpec((TM, CONV_DIM), lambda i: (i, 0)),
        pl.BlockSpec((TM, 3 * D), lambda i: (i, 0)),
        pl.BlockSpec((MLA_HEADS, HEAD_PAD, TM), lambda i: (0, 0, i)),
        pl.BlockSpec((MLA_HEADS, TM, HEAD_PAD), lambda i: (0, i, 0)),
        pl.BlockSpec((1, MLA_HEADS * MLA_V, KC), lambda i: (i, 0, 0)),
        pl.BlockSpec((MLA_HEADS, TM), lambda i: (0, i)),
        pl.BlockSpec((GQA_HEADS, HEAD_PAD, TM), lambda i: (0, 0, i)),
        pl.BlockSpec((TM, HEAD_PAD), lambda i: (i, 0)),
        pl.BlockSpec((1, GQA_KV_HEADS * GQA_HEAD_DIM, KC), lambda i: (i, 0, 0)),
        pl.BlockSpec((GQA_KV_HEADS, TM), lambda i: (0, i)),
    ]
    out_shape = [
        jax.ShapeDtypeStruct((NA, CONV_DIM), F32),
        jax.ShapeDtypeStruct((NA, 3 * D), F32),
        jax.ShapeDtypeStruct((MLA_HEADS, HEAD_PAD, NA), BF),
        jax.ShapeDtypeStruct((MLA_HEADS, NA, HEAD_PAD), BF),
        jax.ShapeDtypeStruct((NT, MLA_HEADS * MLA_V, KC), BF),
        jax.ShapeDtypeStruct((MLA_HEADS, NA), F32),
        jax.ShapeDtypeStruct((GQA_HEADS, HEAD_PAD, NA), BF),
        jax.ShapeDtypeStruct((NA, HEAD_PAD), BF),
        jax.ShapeDtypeStruct((NT, GQA_KV_HEADS * GQA_HEAD_DIM, KC), BF),
        jax.ShapeDtypeStruct((GQA_KV_HEADS, NA), F32),
    ]
    return pl.pallas_call(
        _proj_kernel,
        grid=(NT,),
        in_specs=in_specs,
        out_specs=out_specs,
        out_shape=out_shape,
        compiler_params=pltpu.CompilerParams(dimension_semantics=("arbitrary",),
                                             vmem_limit_bytes=VMEM_LIMIT),
        name="proj",
    )(xall, mod_l, gn, wmain, wfm, wqb, wkvb, bgate, gqa, gkva, gmq, gmk, ggq, ggk,
      cosm, sinm, cosg, sing)


def _attn_kernel(q_ref, k_ref, vt_ref, kn_ref, o_ref, *, k_per_head, v_per_head):
    j = pl.program_id(1)

    def k_head(hh):
        return hh if k_per_head else 0

    def v_rows(hh):
        v0 = hh * MLA_V if v_per_head else 0
        return slice(v0, v0 + MLA_V)

    def write(outs):
        o_ref[...] = jnp.concatenate(outs, axis=0).T.astype(o_ref.dtype)

    def scores(hh, c):
        start = pl.multiple_of(c * KC, KC)
        return _dot(k_ref[k_head(hh), pl.ds(start, KC), :], q_ref[hh])

    def pv(hh, c, p):
        return _dot(vt_ref[c, v_rows(hh), :], p.astype(BF))

    def online(hh, n_chunks):
        s = scores(hh, 0)
        m = jnp.max(s, axis=0, keepdims=True)
        p = jnp.exp2(s - m)
        carry = (m, jnp.sum(p, axis=0, keepdims=True), pv(hh, 0, p))

        def body(c, carry):
            m, den, acc = carry
            s = scores(hh, c)
            m_new = jnp.maximum(m, jnp.max(s, axis=0, keepdims=True))
            alpha = jnp.exp2(m - m_new)
            p = jnp.exp2(s - m_new)
            return m_new, alpha * den + jnp.sum(p, axis=0, keepdims=True), alpha * acc + pv(hh, c, p)

        _, den, acc = lax.fori_loop(1, n_chunks, body, carry)
        return acc / den

    @pl.when(j == 0)
    def _():
        write([online(hh, 1) for hh in range(2)])

    @pl.when(j > 0)
    def _():
        shift = []
        for hh in range(2):
            qf = q_ref[hh].astype(F32)
            qn2 = jnp.sum(qf * qf, axis=0, keepdims=True)
            kn2 = jnp.max(jnp.max(kn_ref[k_head(hh)], axis=0, keepdims=True), axis=1, keepdims=True)
            shift.append(jnp.sqrt(qn2 * kn2))
        acc = [jnp.zeros((MLA_V, MQ), F32)] * 2
        den = [jnp.zeros((1, MQ), F32)] * 2
        p_prev = [None, None]
        rows = QK_GROUP * KC
        for g in range(NT // QK_GROUP + 1):
            p_new = [None, None]
            if g < NT // QK_GROUP:
                for hh in range(2):
                    s = _dot(k_ref[k_head(hh), g * rows:(g + 1) * rows, :], q_ref[hh])
                    e = jnp.exp2(s - shift[hh])
                    den[hh] = den[hh] + jnp.sum(e, axis=0, keepdims=True)
                    p_new[hh] = e.astype(BF)
            if g >= 1:
                for hh in range(2):
                    for u in range(QK_GROUP):
                        acc[hh] = acc[hh] + _dot(vt_ref[(g - 1) * QK_GROUP + u, v_rows(hh), :],
                                                 p_prev[hh][u * KC:(u + 1) * KC])
            p_prev = p_new
        write([a / d for a, d in zip(acc, den)])
        den_min = jnp.min(jnp.minimum(den[0], den[1]))

        @pl.when(jnp.logical_not(den_min >= DEN_FLOOR))
        def _():
            write([online(hh, NT) for hh in range(2)])


def _attn_call(qt, k, vt, kn, *, k_per_head, v_per_head, name):
    if k_per_head:
        k_spec = pl.BlockSpec((2, NA, HEAD_PAD), lambda p, j: (p, 0, 0))
        kn_spec = pl.BlockSpec((2, NA // 128, 128), lambda p, j: (p, 0, 0))
    else:
        k_spec = pl.BlockSpec((1, NA, HEAD_PAD), lambda p, j: (0, 0, 0))
        kn_spec = pl.BlockSpec((1, NA // 128, 128), lambda p, j: (p // (GQA_GROUP // 2), 0, 0))
    if v_per_head:
        v_spec = pl.BlockSpec((NT, 2 * MLA_V, KC), lambda p, j: (0, p, 0))
    else:
        v_spec = pl.BlockSpec((NT, GQA_HEAD_DIM, KC), lambda p, j: (0, p // (GQA_GROUP // 2), 0))
    return pl.pallas_call(
        partial(_attn_kernel, k_per_head=k_per_head, v_per_head=v_per_head),
        grid=(4, NA // MQ),
        in_specs=[pl.BlockSpec((2, HEAD_PAD, MQ), lambda p, j: (p, 0, j)), k_spec, v_spec, kn_spec],
        out_specs=pl.BlockSpec((MQ, 2 * MLA_V), lambda p, j: (j, p)),
        out_shape=jax.ShapeDtypeStruct((NA, 8 * MLA_V), BF),
        compiler_params=pltpu.CompilerParams(dimension_semantics=("arbitrary", "arbitrary"),
                                             vmem_limit_bytes=VMEM_LIMIT),
        name=name,
    )(qt, k, vt, kn)


def _merge_kernel(x_ref, yp_ref, yc_ref, yn_ref, gates_ref, om_ref, og_ref, mod_ref,
                  cw_ref, cb_ref, lng_ref, lnb_ref, wco_ref, wmo_ref, wgo_ref, wout_ref,
                  o_ref, buf_ref, sh_ref):
    i = pl.program_id(0)
    left_ok = i >= 2
    right_ok = jnp.logical_and(i >= 1, i < NT - 1)
    buf_ref[0:HALO_Y, :] = jnp.where(left_ok, yp_ref[...], 0.0)
    buf_ref[HALO_Y:HALO_Y + TM, :] = yc_ref[...]
    buf_ref[HALO_Y + TM:HALO_Y + TM + HALO_Y, :] = jnp.where(right_ok, yn_ref[...], 0.0)

    for r in range(1, SUBLANES):
        sh_ref[r - 1] = buf_ref[r:r + CONV_SPAN, :]
    off = HALO_Y - CONV_WIDTH // 2
    acc = jnp.zeros((TM, CONV_DIM), F32) + cb_ref[...]
    for k in range(CONV_WIDTH):
        a, r = divmod(off + k, SUBLANES)
        src = buf_ref if r == 0 else sh_ref.at[r - 1]
        acc = acc + src[a * SUBLANES:a * SUBLANES + TM, :] * cw_ref[k:k + 1, :]
    mu = jnp.mean(acc, axis=-1, keepdims=True)
    xc = acc - mu
    var = jnp.mean(xc * xc, axis=-1, keepdims=True)
    ln = xc * lax.rsqrt(var + EPS) * lng_ref[...] + lnb_ref[...]
    act = (ln * _sigmoid(ln)).astype(BF)
    br_conv = _dot(act, wco_ref[...])
    br_mla = _dot(om_ref[...], wmo_ref[...])
    br_gqa = _dot(og_ref[...], wgo_ref[...])
    merged = (gates_ref[:, 0:D] * br_conv + gates_ref[:, D:2 * D] * br_mla
              + gates_ref[:, 2 * D:3 * D] * br_gqa)
    res = _dot(merged.astype(BF), wout_ref[...])
    o_ref[...] = x_ref[...] + mod_ref[0, 2:3, :] * res


def _merge_call(xall, y, gates, om, og, mod_l, cw, cb, lng, lnb, wco, wmo, wgo, wout):
    ny = NA // HALO_Y
    per = TM // HALO_Y
    in_specs = [
        pl.BlockSpec((TM, D), lambda i: (i, 0)),
        pl.BlockSpec((HALO_Y, CONV_DIM), lambda i: (jnp.maximum(i * per - 1, 0), 0)),
        pl.BlockSpec((TM, CONV_DIM), lambda i: (i, 0)),
        pl.BlockSpec((HALO_Y, CONV_DIM), lambda i: (jnp.minimum((i + 1) * per, ny - 1), 0)),
        pl.BlockSpec((TM, 3 * D), lambda i: (i, 0)),
        pl.BlockSpec((TM, 8 * MLA_V), lambda i: (i, 0)),
        pl.BlockSpec((TM, 8 * MLA_V), lambda i: (i, 0)),
        pl.BlockSpec((1, N_MOD, D), lambda i: (jnp.where(i == 0, 1, 0), 0, 0)),
        _const_spec((CONV_WIDTH, CONV_DIM)),
        _const_spec((1, CONV_DIM)),
        _const_spec((1, CONV_DIM)),
        _const_spec((1, CONV_DIM)),
        _const_spec((CONV_DIM, D)),
        _const_spec((8 * MLA_V, D)),
        _const_spec((8 * GQA_HEAD_DIM, D)),
        _const_spec((D, D)),
    ]
    return pl.pallas_call(
        _merge_kernel,
        grid=(NT,),
        in_specs=in_specs,
        out_specs=pl.BlockSpec((TM, D), lambda i: (i, 0)),
        out_shape=jax.ShapeDtypeStruct((NA, D), F32),
        scratch_shapes=[pltpu.VMEM((TM + 2 * HALO_Y, CONV_DIM), F32),
                        pltpu.VMEM((SUBLANES - 1, CONV_SPAN, CONV_DIM), F32)],
        compiler_params=pltpu.CompilerParams(dimension_semantics=("arbitrary",),
                                             vmem_limit_bytes=VMEM_LIMIT),
        name="merge",
    )(xall, y, y, y, gates, om, og, mod_l, cw, cb, lng, lnb, wco, wmo, wgo, wout)


FF_CHUNK = D_FF


def _ffn_kernel(xp_ref, xc_ref, xn_ref, mod_ref, gn_ref, wup_ref, dw_ref, db_ref, wdown_ref,
                o_ref, ua_ref, ug_ref):
    i = pl.program_id(0)
    left_ok = i >= 2
    right_ok = jnp.logical_and(i >= 1, i < NT - 1)
    xc = xc_ref[...]
    xh = jnp.concatenate([xp_ref[...], xc, xn_ref[...]], axis=0)
    shift = mod_ref[0, 3:4, :]
    scale = mod_ref[0, 4:5, :]
    h = xh * lax.rsqrt(jnp.mean(xh * xh, axis=-1, keepdims=True) + EPS) * gn_ref[...]
    hb = (h * (1.0 + scale) + shift).astype(BF)

    rows = TM + 2 * HALO_X
    row_id = lax.broadcasted_iota(jnp.int32, (rows, 1), 0)
    keep = jnp.logical_and(jnp.logical_or(row_id >= HALO_X, left_ok),
                           jnp.logical_or(row_id < HALO_X + TM, right_ok))

    def conv3(u_ref, col0, width):
        w = dw_ref[:, col0:col0 + width]
        return (u_ref[HALO_X - 1:HALO_X - 1 + TM, :] * w[0:1]
                + u_ref[HALO_X:HALO_X + TM, :] * w[1:2]
                + u_ref[HALO_X + 1:HALO_X + 1 + TM, :] * w[2:3]
                + db_ref[:, col0:col0 + width])

    acc = jnp.zeros((TM, D), F32)
    for c in range(D_FF // FF_CHUNK):
        a0 = c * FF_CHUNK
        g0 = D_FF + c * FF_CHUNK
        ua_ref[...] = jnp.where(keep, _dot(hb, wup_ref[:, a0:a0 + FF_CHUNK]), 0.0)
        ug_ref[...] = jnp.where(keep, _dot(hb, wup_ref[:, g0:g0 + FF_CHUNK]), 0.0)
        a = conv3(ua_ref, a0, FF_CHUNK)
        g = conv3(ug_ref, g0, FF_CHUNK)
        act = (g * _sigmoid(g) * a).astype(BF)
        acc = acc + _dot(act, wdown_ref[a0:a0 + FF_CHUNK, :])
    o_ref[...] = xc + mod_ref[0, 5:6, :] * acc


def _ffn_call(x1, mod_l, gn, wup, dw, db, wdown):
    nx = NA // HALO_X
    per = TM // HALO_X
    in_specs = [
        pl.BlockSpec((HALO_X, D), lambda i: (jnp.maximum(i * per - 1, 0), 0)),
        pl.BlockSpec((TM, D), lambda i: (i, 0)),
        pl.BlockSpec((HALO_X, D), lambda i: (jnp.minimum((i + 1) * per, nx - 1), 0)),
        pl.BlockSpec((1, N_MOD, D), lambda i: (jnp.where(i == 0, 1, 0), 0, 0)),
        _const_spec((1, D)),
        _const_spec((D, 2 * D_FF)),
        _const_spec((3, 2 * D_FF)),
        _const_spec((1, 2 * D_FF)),
        _const_spec((D_FF, D)),
    ]
    return pl.pallas_call(
        _ffn_kernel,
        grid=(NT,),
        in_specs=in_specs,
        out_specs=pl.BlockSpec((TM, D), lambda i: (i, 0)),
        out_shape=jax.ShapeDtypeStruct((NA, D), F32),
        scratch_shapes=[pltpu.VMEM((TM + 2 * HALO_X, FF_CHUNK), F32),
                        pltpu.VMEM((TM + 2 * HALO_X, FF_CHUNK), F32)],
        compiler_params=pltpu.CompilerParams(dimension_semantics=("arbitrary",),
                                             vmem_limit_bytes=VMEM_LIMIT),
        name="ffn",
    )(x1, x1, x1, mod_l, gn, wup, dw, db, wdown)


def _rope_tables(rot_dim):
    n_freq = rot_dim // 4
    inv = 1.0 / (ROPE_THETA ** (jnp.arange(n_freq, dtype=F32) / n_freq))
    pos = jnp.arange(SEQ, dtype=jnp.int32)
    ang_r = (pos // GRID_W).astype(F32)[None, :] * inv[:, None]
    ang_c = (pos % GRID_W).astype(F32)[None, :] * inv[:, None]
    ang = jnp.concatenate([ang_r, ang_r, ang_c, ang_c], axis=0)
    cos = jnp.concatenate([jnp.ones((rot_dim, CTX), F32), jnp.cos(ang)], axis=1)
    sin = jnp.concatenate([jnp.zeros((rot_dim, CTX), F32), jnp.sin(ang)], axis=1)
    return cos, sin


def _pad_col(g, n):
    return jnp.pad(g, (0, n - g.shape[0])).reshape(n, 1)


def kernel(x, c, ctx, c_ctx, w_mod, b_mod, g_norm1, g_norm2, w_in, b_gate, conv_dw_w, conv_dw_b, conv_ln_g, conv_ln_b, w_conv_out, g_q_a, w_q_b, g_kv_a, w_kv_b, g_mla_q, g_mla_k, w_mla_o, g_gqa_q, g_gqa_k, w_gqa_o, w_out, w_up, ffn_dw_w, ffn_dw_b, w_down):
    assert x.shape == (1, SEQ, D) and ctx.shape == (1, CTX, D)
    xall = jnp.concatenate([ctx[0], x[0]], axis=0)
    cond_t = jnp.stack([c[0], c_ctx], axis=1)
    mod = _mod_call(cond_t, w_mod, b_mod).reshape(DEPTH, 2, N_MOD, D)
    cosm, sinm = _rope_tables(MLA_ROPE)
    cosg, sing = _rope_tables(GQA_HEAD_DIM)

    o_qa = 2 * CONV_DIM
    o_kva = o_qa + MLA_Q_RANK
    o_kr = o_kva + MLA_KV_RANK
    o_gq = o_kr + MLA_ROPE
    o_gk = o_gq + GQA_HEADS * GQA_HEAD_DIM
    o_gv = o_gk + GQA_KV_HEADS * GQA_HEAD_DIM
    o_gate = o_gv + GQA_KV_HEADS * GQA_HEAD_DIM

    for l in range(DEPTH):
        wi = w_in[l]
        wmain = jnp.concatenate([wi[:, :o_qa], wi[:, o_gate:], wi[:, o_qa:o_kr]], axis=1).astype(BF)
        wfm = jnp.concatenate([wi[:, o_gq:o_gate], wi[:, o_kr:o_gq]], axis=1).T.astype(BF)
        wqb = jnp.pad(w_q_b[l].reshape(MLA_Q_RANK, MLA_HEADS, MLA_QK),
                      ((0, 0), (0, 0), (0, HEAD_PAD - MLA_QK)))
        wqb = wqb.reshape(MLA_Q_RANK, MLA_HEADS * HEAD_PAD).T.astype(BF)
        wkvb = w_kv_b[l].T.astype(BF)
        mod_l = mod[l]

        y, gates, qtm, km, vtm, knm, qtg, kg, vtg, kng = _proj_call(
            xall, mod_l, g_norm1[l].reshape(1, D), wmain, wfm, wqb, wkvb, b_gate[l].reshape(1, 3 * D),
            g_q_a[l].reshape(1, -1), g_kv_a[l].reshape(1, -1),
            _pad_col(g_mla_q[l], HEAD_PAD), _pad_col(g_mla_k[l], HEAD_PAD),
            g_gqa_q[l].reshape(-1, 1), g_gqa_k[l].reshape(-1, 1), cosm, sinm, cosg, sing)
        om = _attn_call(qtm, km, vtm, knm.reshape(MLA_HEADS, NA // 128, 128),
                        k_per_head=True, v_per_head=True, name="attn_mla")
        og = _attn_call(qtg, kg.reshape(1, NA, HEAD_PAD), vtg, kng.reshape(GQA_KV_HEADS, NA // 128, 128),
                        k_per_head=False, v_per_head=False, name="attn_gqa")
        x1 = _merge_call(xall, y, gates, om, og, mod_l, conv_dw_w[l], conv_dw_b[l].reshape(1, -1),
                         conv_ln_g[l].reshape(1, -1), conv_ln_b[l].reshape(1, -1),
                         w_conv_out[l].astype(BF), w_mla_o[l].astype(BF), w_gqa_o[l].astype(BF),
                         w_out[l].astype(BF))
        xall = _ffn_call(x1, mod_l, g_norm2[l].reshape(1, D), w_up[l].astype(BF), ffn_dw_w[l],
                         ffn_dw_b[l].reshape(1, -1), w_down[l].astype(BF))
    return xall[CTX:][None]
```

```python
import functools
from functools import partial

import jax
import jax.numpy as jnp
from jax import lax
from jax.experimental import pallas as pl
from jax.experimental.pallas import tpu as pltpu

D = 1024
SEQ = 16384
DEPTH = 4
GRID_W = 64
CTX = 256
NA = CTX + SEQ
ROPE_THETA = 10000.0
EPS = 1e-6
N_MOD = 6

CONV_DIM = 512
CONV_WIDTH = 31
MLA_HEADS = 8
MLA_Q_RANK = 384
MLA_KV_RANK = 256
MLA_NOPE = 64
MLA_ROPE = 32
MLA_V = 64
MLA_QK = MLA_NOPE + MLA_ROPE
GQA_HEADS = 8
GQA_KV_HEADS = 2
GQA_HEAD_DIM = 64
GQA_GROUP = GQA_HEADS // GQA_KV_HEADS
D_FF = 2816
LOG2E = 1.4426950408889634
MLA_SCALE = MLA_QK ** -0.5 * LOG2E
GQA_SCALE = GQA_HEAD_DIM ** -0.5 * LOG2E

LANES = 128
HEAD_PAD = LANES
TM = 256
NT = NA // TM
KC = 256
MQ = 256
HPS = 2
QK_GROUP = 13
DEN_FLOOR = 2.0 ** -60
HALO_Y = 16
HALO_X = 8
SUBLANES = 8
CONV_SPAN = TM + (HALO_Y + CONV_WIDTH // 2) // SUBLANES * SUBLANES
VMEM_LIMIT = 56 * 1024 * 1024

BF = jnp.bfloat16
F32 = jnp.float32

C_GLU = 0
C_GATE = 2 * CONV_DIM
C_QA = C_GATE + 3 * D
C_KVA = C_QA + MLA_Q_RANK
W_MAIN = C_KVA + MLA_KV_RANK
R_GQ = 0
R_GK = GQA_HEADS * GQA_HEAD_DIM
R_GV = R_GK + GQA_KV_HEADS * GQA_HEAD_DIM
R_KR = R_GV + GQA_KV_HEADS * GQA_HEAD_DIM
W_FM = R_KR + MLA_ROPE


def _layer_spec(shape, layer):
    nd = len(shape)
    return pl.BlockSpec((None,) + tuple(shape), lambda *_: (layer,) + (0,) * nd, pipeline_mode=pl.Buffered(1))


def _const_spec(shape):
    nd = len(shape)
    return pl.BlockSpec(shape, lambda *_: (0,) * nd, pipeline_mode=pl.Buffered(1))


def _dot(a, b):
    return jnp.dot(a, b, preferred_element_type=F32)


def _dot_nt(a, b):
    return lax.dot_general(a, b, (((1,), (1,)), ((), ())), preferred_element_type=F32)


def _sigmoid(x):
    return 1.0 / (1.0 + jnp.exp(-x))


def _sq_norm_as_stored(kt):
    kq = kt.astype(BF).astype(F32)
    return jnp.sum(kq * kq, axis=0, keepdims=True)


def _rope_rows(x, cos, sin):
    q = x.shape[0] // 4
    x1, x2, x3, x4 = x[0:q], x[q:2 * q], x[2 * q:3 * q], x[3 * q:4 * q]
    rot = jnp.concatenate([-x2, x1, -x4, x3], axis=0)
    return x * cos + rot * sin


MOD_TN = 1536


def _mod_kernel(cond_ref, w_ref, b_ref, o_ref):
    c = cond_ref[...]
    s = c * _sigmoid(c)
    w = w_ref[0]
    b = b_ref[0]
    for j in range(2):
        o_ref[0, j:j + 1, :] = jnp.sum(s[:, j:j + 1] * w, axis=0, keepdims=True) + b


def _mod_call(cond_t, w_mod, b_mod):
    nl = w_mod.shape[0]
    ncol = N_MOD * D
    return pl.pallas_call(
        _mod_kernel,
        grid=(nl, ncol // MOD_TN),
        in_specs=[
            pl.BlockSpec((D, 2), lambda l, j: (0, 0)),
            pl.BlockSpec((1, D, MOD_TN), lambda l, j: (l, 0, j)),
            pl.BlockSpec((1, 1, MOD_TN), lambda l, j: (l, 0, j)),
        ],
        out_specs=pl.BlockSpec((1, 2, MOD_TN), lambda l, j: (l, 0, j)),
        out_shape=jax.ShapeDtypeStruct((nl, 2, ncol), F32),
        compiler_params=pltpu.CompilerParams(vmem_limit_bytes=VMEM_LIMIT),
        name="adaln_mod",
    )(cond_t, w_mod, b_mod.reshape(nl, 1, ncol))


def _proj_kernel(x_ref, mod_ref, gn_ref, wmain_ref, wfm_ref, wqb_ref, wkvb_ref, bgate_ref,
                 gqa_ref, gkva_ref, gmq_ref, gmk_ref, ggq_ref, ggk_ref,
                 cosm_ref, sinm_ref, cosg_ref, sing_ref,
                 y_ref, gates_ref, qtm_ref, km_ref, vtm_ref, knm_ref, qtg_ref, kg_ref, vtg_ref, kng_ref):
    x = x_ref[...]
    shift = mod_ref[0, 0:1, :]
    scale = mod_ref[0, 1:2, :]
    h = x * lax.rsqrt(jnp.mean(x * x, axis=-1, keepdims=True) + EPS) * gn_ref[...]
    hb = (h * (1.0 + scale) + shift).astype(BF)

    za = _dot(hb, wmain_ref[:, C_GLU:C_GLU + CONV_DIM])
    zg = _dot(hb, wmain_ref[:, C_GLU + CONV_DIM:C_GATE])
    y_ref[...] = za * _sigmoid(zg)
    zgate = _dot(hb, wmain_ref[:, C_GATE:C_QA])
    gates_ref[...] = _sigmoid(zgate + bgate_ref[...])

    zqa = _dot(hb, wmain_ref[:, C_QA:C_KVA])
    qa = (zqa * lax.rsqrt(jnp.mean(zqa * zqa, axis=-1, keepdims=True) + EPS) * gqa_ref[...]).astype(BF)
    zkva = _dot(hb, wmain_ref[:, C_KVA:W_MAIN])
    kva = (zkva * lax.rsqrt(jnp.mean(zkva * zkva, axis=-1, keepdims=True) + EPS) * gkva_ref[...]).astype(BF)
    qt = _dot_nt(wqb_ref[...], qa)
    kvt = _dot_nt(wkvb_ref[...], kva)
    fm = _dot_nt(wfm_ref[...], hb)

    cosm, sinm = cosm_ref[...], sinm_ref[...]
    cosg, sing = cosg_ref[...], sing_ref[...]
    gmq, gmk = gmq_ref[...], gmk_ref[...]
    ggq, ggk = ggq_ref[...], ggk_ref[...]
    kr = fm[R_KR:R_KR + MLA_ROPE]
    kr_ss = jnp.sum(kr * kr, axis=0, keepdims=True)
    zpad = jnp.zeros((HEAD_PAD - MLA_QK, TM), F32)

    for hd in range(MLA_HEADS):
        q = qt[hd * HEAD_PAD:(hd + 1) * HEAD_PAD]
        r = lax.rsqrt(jnp.sum(q * q, axis=0, keepdims=True) * (1.0 / MLA_QK) + EPS)
        qn = q * r * gmq
        qr = _rope_rows(qn[MLA_NOPE:MLA_QK], cosm, sinm)
        qfull = jnp.concatenate([qn[:MLA_NOPE], qr, zpad], axis=0) * MLA_SCALE
        qtm_ref[hd] = qfull.astype(BF)

        base = hd * (MLA_NOPE + MLA_V)
        kn = kvt[base:base + MLA_NOPE]
        vtm_ref[0, hd * MLA_V:(hd + 1) * MLA_V, :] = kvt[base + MLA_NOPE:base + MLA_NOPE + MLA_V].astype(BF)
        rk = lax.rsqrt((jnp.sum(kn * kn, axis=0, keepdims=True) + kr_ss) * (1.0 / MLA_QK) + EPS)
        knn = kn * rk * gmk[:MLA_NOPE]
        krn = _rope_rows(kr * rk * gmk[MLA_NOPE:MLA_QK], cosm, sinm)
        kfull = jnp.concatenate([knn, krn, zpad], axis=0)
        km_ref[hd] = kfull.T.astype(BF)
        knm_ref[hd:hd + 1, :] = _sq_norm_as_stored(kfull)

    zhalf = jnp.zeros((GQA_HEAD_DIM, TM), F32)
    for hd in range(GQA_HEADS):
        q = fm[R_GQ + hd * GQA_HEAD_DIM:R_GQ + (hd + 1) * GQA_HEAD_DIM]
        r = lax.rsqrt(jnp.mean(q * q, axis=0, keepdims=True) + EPS)
        qn = _rope_rows(q * r * ggq, cosg, sing) * GQA_SCALE
        parts = [qn, zhalf] if hd // GQA_GROUP == 0 else [zhalf, qn]
        qtg_ref[hd] = jnp.concatenate(parts, axis=0).astype(BF)
    ks = []
    for g in range(GQA_KV_HEADS):
        k = fm[R_GK + g * GQA_HEAD_DIM:R_GK + (g + 1) * GQA_HEAD_DIM]
        r = lax.rsqrt(jnp.mean(k * k, axis=0, keepdims=True) + EPS)
        ks.append(_rope_rows(k * r * ggk, cosg, sing))
        kng_ref[g:g + 1, :] = _sq_norm_as_stored(ks[g])
    kg_ref[...] = jnp.concatenate(ks, axis=0).T.astype(BF)
    vtg_ref[0] = fm[R_GV:R_KR].astype(BF)


def _proj_call(xall, mod_l, gn, wmain, wfm, wqb, wkvb, bgate, gqa, gkva, gmq, gmk, ggq, ggk,
               cosm, sinm, cosg, sing):
    def stream(i):
        return jnp.where(i == 0, 1, 0)

    in_specs = [
        pl.BlockSpec((TM, D), lambda i: (i, 0)),
        pl.BlockSpec((1, N_MOD, D), lambda i: (stream(i), 0, 0)),
        _const_spec((1, D)),
        _const_spec((D, W_MAIN)),
        _const_spec((W_FM, D)),
        _const_spec((MLA_HEADS * HEAD_PAD, MLA_Q_RANK)),
        _const_spec((MLA_HEADS * (MLA_NOPE + MLA_V), MLA_KV_RANK)),
        _const_spec((1, 3 * D)),
        _const_spec((1, MLA_Q_RANK)),
        _const_spec((1, MLA_KV_RANK)),
        _const_spec((HEAD_PAD, 1)),
        _const_spec((HEAD_PAD, 1)),
        _const_spec((GQA_HEAD_DIM, 1)),
        _const_spec((GQA_HEAD_DIM, 1)),
        pl.BlockSpec((MLA_ROPE, TM), lambda i: (0, i)),
        pl.BlockSpec((MLA_ROPE, TM), lambda i: (0, i)),
        pl.BlockSpec((GQA_HEAD_DIM, TM), lambda i: (0, i)),
        pl.BlockSpec((GQA_HEAD_DIM, TM), lambda i: (0, i)),
    ]
    out_specs = [
        pl.BlockSpec((TM, CONV_DIM), lambda i: (i, 0)),
        pl.BlockSpec((TM, 3 * D), lambda i: (i, 0)),
        pl.BlockSpec((MLA_HEADS, HEAD_PAD, TM), lambda i: (0, 0, i)),
        pl.BlockSpec((MLA_HEADS, TM, HEAD_PAD), lambda i: (0, i, 0)),
        pl.BlockSpec((1, MLA_HEADS * MLA_V, KC), lambda i: (i, 0, 0)),
        pl.BlockSpec((MLA_HEADS, TM), lambda i: (0, i)),
        pl.BlockSpec((GQA_HEADS, HEAD_PAD, TM), lambda i: (0, 0, i)),
        pl.BlockSpec((TM, HEAD_PAD), lambda i: (i, 0)),
        pl.BlockSpec((1, GQA_KV_HEADS * GQA_HEAD_DIM, KC), lambda i: (i, 0, 0)),
        pl.BlockSpec((GQA_KV_HEADS, TM), lambda i: (0, i)),
    ]
    out_shape = [
        jax.ShapeDtypeStruct((NA, CONV_DIM), F32),
        jax.ShapeDtypeStruct((NA, 3 * D), F32),
        jax.ShapeDtypeStruct((MLA_HEADS, HEAD_PAD, NA), BF),
        jax.ShapeDtypeStruct((MLA_HEADS, NA, HEAD_PAD), BF),
        jax.ShapeDtypeStruct((NT, MLA_HEADS * MLA_V, KC), BF),
        jax.ShapeDtypeStruct((MLA_HEADS, NA), F32),
        jax.ShapeDtypeStruct((GQA_HEADS, HEAD_PAD, NA), BF),
        jax.ShapeDtypeStruct((NA, HEAD_PAD), BF),
        jax.ShapeDtypeStruct((NT, GQA_KV_HEADS * GQA_HEAD_DIM, KC), BF),
        jax.ShapeDtypeStruct((GQA_KV_HEADS, NA), F32),
    ]
    return pl.pallas_call(
        _proj_kernel,
        grid=(NT,),
        in_specs=in_specs,
        out_specs=out_specs,
        out_shape=out_shape,
        compiler_params=pltpu.CompilerParams(dimension_semantics=("arbitrary",),
                                             vmem_limit_bytes=VMEM_LIMIT),
        name="proj",
    )(xall, mod_l, gn, wmain, wfm, wqb, wkvb, bgate, gqa, gkva, gmq, gmk, ggq, ggk,
      cosm, sinm, cosg, sing)


def _attn_kernel(q_ref, k_ref, vt_ref, kn_ref, o_ref, *, k_per_head, v_per_head):
    j = pl.program_id(1)

    def k_head(hh):
        return hh if k_per_head else 0

    def v_rows(hh):
        v0 = hh * MLA_V if v_per_head else 0
        return slice(v0, v0 + MLA_V)

    def write(outs):
        o_ref[...] = jnp.concatenate(outs, axis=0).T.astype(o_ref.dtype)

    def scores(hh, c):
        start = pl.multiple_of(c * KC, KC)
        return _dot(k_ref[k_head(hh), pl.ds(start, KC), :], q_ref[hh])

    def pv(hh, c, p):
        return _dot(vt_ref[c, v_rows(hh), :], p.astype(BF))

    def online(hh, n_chunks):
        s = scores(hh, 0)
        m = jnp.max(s, axis=0, keepdims=True)
        p = jnp.exp2(s - m)
        carry = (m, jnp.sum(p, axis=0, keepdims=True), pv(hh, 0, p))

        def body(c, carry):
            m, den, acc = carry
            s = scores(hh, c)
            m_new = jnp.maximum(m, jnp.max(s, axis=0, keepdims=True))
            alpha = jnp.exp2(m - m_new)
            p = jnp.exp2(s - m_new)
            return m_new, alpha * den + jnp.sum(p, axis=0, keepdims=True), alpha * acc + pv(hh, c, p)

        _, den, acc = lax.fori_loop(1, n_chunks, body, carry)
        return acc / den

    @pl.when(j == 0)
    def _():
        write([online(hh, 1) for hh in range(HPS)])

    @pl.when(j > 0)
    def _():
        shift = []
        for hh in range(HPS):
            qf = q_ref[hh].astype(F32)
            qn2 = jnp.sum(qf * qf, axis=0, keepdims=True)
            kn2 = jnp.max(jnp.max(kn_ref[k_head(hh)], axis=0, keepdims=True), axis=1, keepdims=True)
            shift.append(jnp.sqrt(qn2 * kn2))
        acc = [jnp.zeros((MLA_V, MQ), F32)] * HPS
        den = [jnp.zeros((1, MQ), F32)] * HPS
        p_prev = [None] * HPS
        rows = QK_GROUP * KC
        for g in range(NT // QK_GROUP + 1):
            p_new = [None] * HPS
            for hh in range(HPS):
                if g < NT // QK_GROUP:
                    s = _dot(k_ref[k_head(hh), g * rows:(g + 1) * rows, :], q_ref[hh])
                    e = jnp.exp2(s - shift[hh])
                    den[hh] = den[hh] + jnp.sum(e, axis=0, keepdims=True)
                    p_new[hh] = e.astype(BF)
                if g >= 1:
                    for u in range(QK_GROUP):
                        acc[hh] = acc[hh] + _dot(vt_ref[(g - 1) * QK_GROUP + u, v_rows(hh), :],
                                                 p_prev[hh][u * KC:(u + 1) * KC])
            p_prev = p_new
        write([a / d for a, d in zip(acc, den)])
        den_min = jnp.min(functools.reduce(jnp.minimum, den))

        @pl.when(jnp.logical_not(den_min >= DEN_FLOOR))
        def _():
            write([online(hh, NT) for hh in range(HPS)])


def _attn_call(qt, k, vt, kn, *, k_per_head, v_per_head, name):
    if k_per_head:
        k_spec = pl.BlockSpec((HPS, NA, HEAD_PAD), lambda p, j: (p, 0, 0))
        kn_spec = pl.BlockSpec((HPS, NA // LANES, LANES), lambda p, j: (p, 0, 0))
    else:
        k_spec = pl.BlockSpec((1, NA, HEAD_PAD), lambda p, j: (0, 0, 0))
        kn_spec = pl.BlockSpec((1, NA // LANES, LANES), lambda p, j: (p * HPS // GQA_GROUP, 0, 0))
    if v_per_head:
        v_spec = pl.BlockSpec((NT, HPS * MLA_V, KC), lambda p, j: (0, p, 0))
    else:
        v_spec = pl.BlockSpec((NT, GQA_HEAD_DIM, KC), lambda p, j: (0, p * HPS // GQA_GROUP, 0))
    return pl.pallas_call(
        partial(_attn_kernel, k_per_head=k_per_head, v_per_head=v_per_head),
        grid=(MLA_HEADS // HPS, NA // MQ),
        in_specs=[pl.BlockSpec((HPS, HEAD_PAD, MQ), lambda p, j: (p, 0, j)), k_spec, v_spec, kn_spec],
        out_specs=pl.BlockSpec((MQ, HPS * MLA_V), lambda p, j: (j, p)),
        out_shape=jax.ShapeDtypeStruct((NA, 8 * MLA_V), BF),
        compiler_params=pltpu.CompilerParams(dimension_semantics=("arbitrary", "arbitrary"),
                                             vmem_limit_bytes=VMEM_LIMIT),
        name=name,
    )(qt, k, vt, kn)


def _merge_kernel(x_ref, yp_ref, yc_ref, yn_ref, gates_ref, om_ref, og_ref, mod_ref,
                  cw_ref, cb_ref, lng_ref, lnb_ref, wco_ref, wmo_ref, wgo_ref, wout_ref,
                  o_ref, buf_ref, sh_ref):
    i = pl.program_id(0)
    left_ok = i >= 2
    right_ok = jnp.logical_and(i >= 1, i < NT - 1)
    buf_ref[0:HALO_Y, :] = jnp.where(left_ok, yp_ref[...], 0.0)
    buf_ref[HALO_Y:HALO_Y + TM, :] = yc_ref[...]
    buf_ref[HALO_Y + TM:HALO_Y + TM + HALO_Y, :] = jnp.where(right_ok, yn_ref[...], 0.0)

    for r in range(1, SUBLANES):
        sh_ref[r - 1] = buf_ref[r:r + CONV_SPAN, :]
    off = HALO_Y - CONV_WIDTH // 2
    acc = jnp.zeros((TM, CONV_DIM), F32) + cb_ref[...]
    for k in range(CONV_WIDTH):
        a, r = divmod(off + k, SUBLANES)
        src = buf_ref if r == 0 else sh_ref.at[r - 1]
        acc = acc + src[a * SUBLANES:a * SUBLANES + TM, :] * cw_ref[k:k + 1, :]
    mu = jnp.mean(acc, axis=-1, keepdims=True)
    xc = acc - mu
    var = jnp.mean(xc * xc, axis=-1, keepdims=True)
    ln = xc * lax.rsqrt(var + EPS) * lng_ref[...] + lnb_ref[...]
    act = (ln * _sigmoid(ln)).astype(BF)
    br_conv = _dot(act, wco_ref[...])
    br_mla = _dot(om_ref[...], wmo_ref[...])
    br_gqa = _dot(og_ref[...], wgo_ref[...])
    merged = (gates_ref[:, 0:D] * br_conv + gates_ref[:, D:2 * D] * br_mla
              + gates_ref[:, 2 * D:3 * D] * br_gqa)
    res = _dot(merged.astype(BF), wout_ref[...])
    o_ref[...] = x_ref[...] + mod_ref[0, 2:3, :] * res


def _merge_call(xall, y, gates, om, og, mod_l, cw, cb, lng, lnb, wco, wmo, wgo, wout, layer):
    ny = NA // HALO_Y
    per = TM // HALO_Y
    in_specs = [
        pl.BlockSpec((TM, D), lambda i: (i, 0)),
        pl.BlockSpec((HALO_Y, CONV_DIM), lambda i: (jnp.maximum(i * per - 1, 0), 0)),
        pl.BlockSpec((TM, CONV_DIM), lambda i: (i, 0)),
        pl.BlockSpec((HALO_Y, CONV_DIM), lambda i: (jnp.minimum((i + 1) * per, ny - 1), 0)),
        pl.BlockSpec((TM, 3 * D), lambda i: (i, 0)),
        pl.BlockSpec((TM, 8 * MLA_V), lambda i: (i, 0)),
        pl.BlockSpec((TM, 8 * MLA_V), lambda i: (i, 0)),
        pl.BlockSpec((1, N_MOD, D), lambda i: (jnp.where(i == 0, 1, 0), 0, 0)),
        _const_spec((CONV_WIDTH, CONV_DIM)),
        _const_spec((1, CONV_DIM)),
        _const_spec((1, CONV_DIM)),
        _const_spec((1, CONV_DIM)),
        _layer_spec((CONV_DIM, D), layer),
        _layer_spec((8 * MLA_V, D), layer),
        _layer_spec((8 * GQA_HEAD_DIM, D), layer),
        _layer_spec((D, D), layer),
    ]
    return pl.pallas_call(
        _merge_kernel,
        grid=(NT,),
        in_specs=in_specs,
        out_specs=pl.BlockSpec((TM, D), lambda i: (i, 0)),
        out_shape=jax.ShapeDtypeStruct((NA, D), F32),
        scratch_shapes=[pltpu.VMEM((TM + 2 * HALO_Y, CONV_DIM), F32),
                        pltpu.VMEM((SUBLANES - 1, CONV_SPAN, CONV_DIM), F32)],
        compiler_params=pltpu.CompilerParams(dimension_semantics=("arbitrary",),
                                             vmem_limit_bytes=VMEM_LIMIT),
        name="merge",
    )(xall, y, y, y, gates, om, og, mod_l, cw, cb, lng, lnb, wco, wmo, wgo, wout)


FF_CHUNK = D_FF


def _ffn_kernel(xp_ref, xc_ref, xn_ref, mod_ref, gn_ref, wup_ref, dw_ref, db_ref, wdown_ref,
                o_ref, ua_ref, ug_ref):
    i = pl.program_id(0)
    left_ok = i >= 2
    right_ok = jnp.logical_and(i >= 1, i < NT - 1)
    xc = xc_ref[...]
    xh = jnp.concatenate([xp_ref[...], xc, xn_ref[...]], axis=0)
    shift = mod_ref[0, 3:4, :]
    scale = mod_ref[0, 4:5, :]
    h = xh * lax.rsqrt(jnp.mean(xh * xh, axis=-1, keepdims=True) + EPS) * gn_ref[...]
    hb = (h * (1.0 + scale) + shift).astype(BF)

    rows = TM + 2 * HALO_X
    row_id = lax.broadcasted_iota(jnp.int32, (rows, 1), 0)
    keep = jnp.logical_and(jnp.logical_or(row_id >= HALO_X, left_ok),
                           jnp.logical_or(row_id < HALO_X + TM, right_ok))

    def conv3(u_ref, col0, width):
        w = dw_ref[:, col0:col0 + width]
        return (u_ref[HALO_X - 1:HALO_X - 1 + TM, :] * w[0:1]
                + u_ref[HALO_X:HALO_X + TM, :] * w[1:2]
                + u_ref[HALO_X + 1:HALO_X + 1 + TM, :] * w[2:3]
                + db_ref[:, col0:col0 + width])

    acc = jnp.zeros((TM, D), F32)
    for c in range(D_FF // FF_CHUNK):
        a0 = c * FF_CHUNK
        g0 = D_FF + c * FF_CHUNK
        ug_ref[...] = jnp.where(keep, _dot(hb, wup_ref[:, g0:g0 + FF_CHUNK]), 0.0)
        ua_ref[...] = jnp.where(keep, _dot(hb, wup_ref[:, a0:a0 + FF_CHUNK]), 0.0)
        g = conv3(ug_ref, g0, FF_CHUNK)
        gate = g * _sigmoid(g)
        a = conv3(ua_ref, a0, FF_CHUNK)
        act = (gate * a).astype(BF)
        acc = acc + _dot(act, wdown_ref[a0:a0 + FF_CHUNK, :])
    o_ref[...] = xc + mod_ref[0, 5:6, :] * acc


def _ffn_call(x1, mod_l, gn, wup, dw, db, wdown, layer):
    nx = NA // HALO_X
    per = TM // HALO_X
    in_specs = [
        pl.BlockSpec((HALO_X, D), lambda i: (jnp.maximum(i * per - 1, 0), 0)),
        pl.BlockSpec((TM, D), lambda i: (i, 0)),
        pl.BlockSpec((HALO_X, D), lambda i: (jnp.minimum((i + 1) * per, nx - 1), 0)),
        pl.BlockSpec((1, N_MOD, D), lambda i: (jnp.where(i == 0, 1, 0), 0, 0)),
        _const_spec((1, D)),
        _layer_spec((D, 2 * D_FF), layer),
        _const_spec((3, 2 * D_FF)),
        _const_spec((1, 2 * D_FF)),
        _layer_spec((D_FF, D), layer),
    ]
    return pl.pallas_call(
        _ffn_kernel,
        grid=(NT,),
        in_specs=in_specs,
        out_specs=pl.BlockSpec((TM, D), lambda i: (i, 0)),
        out_shape=jax.ShapeDtypeStruct((NA, D), F32),
        scratch_shapes=[pltpu.VMEM((TM + 2 * HALO_X, FF_CHUNK), F32),
                        pltpu.VMEM((TM + 2 * HALO_X, FF_CHUNK), F32)],
        compiler_params=pltpu.CompilerParams(dimension_semantics=("arbitrary",),
                                             vmem_limit_bytes=VMEM_LIMIT),
        name="ffn",
    )(x1, x1, x1, mod_l, gn, wup, dw, db, wdown)


def _rope_tables(rot_dim):
    n_freq = rot_dim // 4
    inv = 1.0 / (ROPE_THETA ** (jnp.arange(n_freq, dtype=F32) / n_freq))
    pos = jnp.arange(SEQ, dtype=jnp.int32)
    ang_r = (pos // GRID_W).astype(F32)[None, :] * inv[:, None]
    ang_c = (pos % GRID_W).astype(F32)[None, :] * inv[:, None]
    ang = jnp.concatenate([ang_r, ang_r, ang_c, ang_c], axis=0)
    cos = jnp.concatenate([jnp.ones((rot_dim, CTX), F32), jnp.cos(ang)], axis=1)
    sin = jnp.concatenate([jnp.zeros((rot_dim, CTX), F32), jnp.sin(ang)], axis=1)
    return cos, sin


def _pad_col(g, n):
    return jnp.pad(g, (0, n - g.shape[0])).reshape(n, 1)


def kernel(x, c, ctx, c_ctx, w_mod, b_mod, g_norm1, g_norm2, w_in, b_gate, conv_dw_w, conv_dw_b, conv_ln_g, conv_ln_b, w_conv_out, g_q_a, w_q_b, g_kv_a, w_kv_b, g_mla_q, g_mla_k, w_mla_o, g_gqa_q, g_gqa_k, w_gqa_o, w_out, w_up, ffn_dw_w, ffn_dw_b, w_down):
    assert x.shape == (1, SEQ, D) and ctx.shape == (1, CTX, D)
    xall = jnp.concatenate([ctx[0], x[0]], axis=0)
    cond_t = jnp.stack([c[0], c_ctx], axis=1)
    mod = _mod_call(cond_t, w_mod, b_mod).reshape(DEPTH, 2, N_MOD, D)
    cosm, sinm = _rope_tables(MLA_ROPE)
    cosg, sing = _rope_tables(GQA_HEAD_DIM)

    o_qa = 2 * CONV_DIM
    o_kva = o_qa + MLA_Q_RANK
    o_kr = o_kva + MLA_KV_RANK
    o_gq = o_kr + MLA_ROPE
    o_gk = o_gq + GQA_HEADS * GQA_HEAD_DIM
    o_gv = o_gk + GQA_KV_HEADS * GQA_HEAD_DIM
    o_gate = o_gv + GQA_KV_HEADS * GQA_HEAD_DIM

    wco_b, wmo_b, wgo_b, wout_b = (w.astype(BF) for w in (w_conv_out, w_mla_o, w_gqa_o, w_out))
    wup_b, wdown_b = w_up.astype(BF), w_down.astype(BF)

    for l in range(DEPTH):
        wi = w_in[l]
        wmain = jnp.concatenate([wi[:, :o_qa], wi[:, o_gate:], wi[:, o_qa:o_kr]], axis=1).astype(BF)
        wfm = jnp.concatenate([wi[:, o_gq:o_gate], wi[:, o_kr:o_gq]], axis=1).T.astype(BF)
        wqb = jnp.pad(w_q_b[l].reshape(MLA_Q_RANK, MLA_HEADS, MLA_QK),
                      ((0, 0), (0, 0), (0, HEAD_PAD - MLA_QK)))
        wqb = wqb.reshape(MLA_Q_RANK, MLA_HEADS * HEAD_PAD).T.astype(BF)
        wkvb = w_kv_b[l].T.astype(BF)
        mod_l = mod[l]

        y, gates, qtm, km, vtm, knm, qtg, kg, vtg, kng = _proj_call(
            xall, mod_l, g_norm1[l].reshape(1, D), wmain, wfm, wqb, wkvb, b_gate[l].reshape(1, 3 * D),
            g_q_a[l].reshape(1, -1), g_kv_a[l].reshape(1, -1),
            _pad_col(g_mla_q[l], HEAD_PAD), _pad_col(g_mla_k[l], HEAD_PAD),
            g_gqa_q[l].reshape(-1, 1), g_gqa_k[l].reshape(-1, 1), cosm, sinm, cosg, sing)
        om = _attn_call(qtm, km, vtm, knm.reshape(MLA_HEADS, NA // LANES, LANES),
                        k_per_head=True, v_per_head=True, name="attn_mla")
        og = _attn_call(qtg, kg.reshape(1, NA, HEAD_PAD), vtg, kng.reshape(GQA_KV_HEADS, NA // LANES, LANES),
                        k_per_head=False, v_per_head=False, name="attn_gqa")
        x1 = _merge_call(xall, y, gates, om, og, mod_l, conv_dw_w[l], conv_dw_b[l].reshape(1, -1),
                         conv_ln_g[l].reshape(1, -1), conv_ln_b[l].reshape(1, -1),
                         wco_b, wmo_b, wgo_b, wout_b, l)
        xall = _ffn_call(x1, mod_l, g_norm2[l].reshape(1, D), wup_b, ffn_dw_w[l],
                         ffn_dw_b[l].reshape(1, -1), wdown_b, l)
    return xall[CTX:][None]
```

```python
import functools
from functools import partial

import jax
import jax.numpy as jnp
from jax import lax
from jax.experimental import pallas as pl
from jax.experimental.pallas import tpu as pltpu

D = 1024
SEQ = 16384
DEPTH = 4
GRID_W = 64
CTX = 256
NA = CTX + SEQ
ROPE_THETA = 10000.0
EPS = 1e-6
N_MOD = 6

CONV_DIM = 512
CONV_WIDTH = 31
MLA_HEADS = 8
MLA_Q_RANK = 384
MLA_KV_RANK = 256
MLA_NOPE = 64
MLA_ROPE = 32
MLA_V = 64
MLA_QK = MLA_NOPE + MLA_ROPE
GQA_HEADS = 8
GQA_KV_HEADS = 2
GQA_HEAD_DIM = 64
GQA_GROUP = GQA_HEADS // GQA_KV_HEADS
D_FF = 2816
LOG2E = 1.4426950408889634
MLA_SCALE = MLA_QK ** -0.5 * LOG2E
GQA_SCALE = GQA_HEAD_DIM ** -0.5 * LOG2E

LANES = 128
HEAD_PAD = LANES
TM = 256
NT = NA // TM
KC = 256
MQ = 256
HPS = 2
QK_GROUP = 13
DEN_FLOOR = 2.0 ** -60
HALO_Y = 16
HALO_X = 8
SUBLANES = 8
CONV_SPAN = TM + (HALO_Y + CONV_WIDTH // 2) // SUBLANES * SUBLANES
VMEM_LIMIT = 56 * 1024 * 1024

BF = jnp.bfloat16
F32 = jnp.float32

C_GLU = 0
C_GATE = 2 * CONV_DIM
C_QA = C_GATE + 3 * D
C_KVA = C_QA + MLA_Q_RANK
W_MAIN = C_KVA + MLA_KV_RANK
R_GQ = 0
R_GK = GQA_HEADS * GQA_HEAD_DIM
R_GV = R_GK + GQA_KV_HEADS * GQA_HEAD_DIM
R_KR = R_GV + GQA_KV_HEADS * GQA_HEAD_DIM
W_FM = R_KR + MLA_ROPE


def _layer_spec(shape, layer):
    nd = len(shape)
    return pl.BlockSpec((None,) + tuple(shape), lambda *_: (layer,) + (0,) * nd, pipeline_mode=pl.Buffered(1))


def _const_spec(shape):
    nd = len(shape)
    return pl.BlockSpec(shape, lambda *_: (0,) * nd, pipeline_mode=pl.Buffered(1))


def _dot(a, b):
    return jnp.dot(a, b, preferred_element_type=F32)


def _dot_nt(a, b):
    return lax.dot_general(a, b, (((1,), (1,)), ((), ())), preferred_element_type=F32)


def _sigmoid(x):
    return 1.0 / (1.0 + jnp.exp(-x))


def _sq_norm_as_stored(kt):
    kq = kt.astype(BF).astype(F32)
    return jnp.sum(kq * kq, axis=0, keepdims=True)


def _rope_rows(x, cos, sin):
    q = x.shape[0] // 4
    x1, x2, x3, x4 = x[0:q], x[q:2 * q], x[2 * q:3 * q], x[3 * q:4 * q]
    rot = jnp.concatenate([-x2, x1, -x4, x3], axis=0)
    return x * cos + rot * sin


MOD_TN = 1536


def _mod_kernel(cond_ref, w_ref, b_ref, o_ref):
    c = cond_ref[...]
    s = c * _sigmoid(c)
    w = w_ref[0]
    b = b_ref[0]
    for j in range(2):
        o_ref[0, j:j + 1, :] = jnp.sum(s[:, j:j + 1] * w, axis=0, keepdims=True) + b


def _mod_call(cond_t, w_mod, b_mod):
    nl = w_mod.shape[0]
    ncol = N_MOD * D
    return pl.pallas_call(
        _mod_kernel,
        grid=(nl, ncol // MOD_TN),
        in_specs=[
            pl.BlockSpec((D, 2), lambda l, j: (0, 0)),
            pl.BlockSpec((1, D, MOD_TN), lambda l, j: (l, 0, j)),
            pl.BlockSpec((1, 1, MOD_TN), lambda l, j: (l, 0, j)),
        ],
        out_specs=pl.BlockSpec((1, 2, MOD_TN), lambda l, j: (l, 0, j)),
        out_shape=jax.ShapeDtypeStruct((nl, 2, ncol), F32),
        compiler_params=pltpu.CompilerParams(vmem_limit_bytes=VMEM_LIMIT),
        name="adaln_mod",
    )(cond_t, w_mod, b_mod.reshape(nl, 1, ncol))


def _proj_kernel(x_ref, mod_ref, gn_ref, wmain_ref, wfm_ref, wqb_ref, wkvb_ref, bgate_ref,
                 gqa_ref, gkva_ref, gmq_ref, gmk_ref, ggq_ref, ggk_ref,
                 cosm_ref, sinm_ref, cosg_ref, sing_ref,
                 y_ref, gates_ref, qtm_ref, km_ref, vtm_ref, knm_ref, qtg_ref, kg_ref, vtg_ref, kng_ref):
    x = x_ref[...]
    shift = mod_ref[0, 0:1, :]
    scale = mod_ref[0, 1:2, :]
    h = x * lax.rsqrt(jnp.mean(x * x, axis=-1, keepdims=True) + EPS) * gn_ref[...]
    hb = (h * (1.0 + scale) + shift).astype(BF)

    za = _dot(hb, wmain_ref[:, C_GLU:C_GLU + CONV_DIM])
    zg = _dot(hb, wmain_ref[:, C_GLU + CONV_DIM:C_GATE])
    y_ref[...] = za * _sigmoid(zg)
    zgate = _dot(hb, wmain_ref[:, C_GATE:C_QA])
    gates_ref[...] = _sigmoid(zgate + bgate_ref[...])

    zqa = _dot(hb, wmain_ref[:, C_QA:C_KVA])
    qa = (zqa * lax.rsqrt(jnp.mean(zqa * zqa, axis=-1, keepdims=True) + EPS) * gqa_ref[...]).astype(BF)
    zkva = _dot(hb, wmain_ref[:, C_KVA:W_MAIN])
    kva = (zkva * lax.rsqrt(jnp.mean(zkva * zkva, axis=-1, keepdims=True) + EPS) * gkva_ref[...]).astype(BF)
    qt = _dot_nt(wqb_ref[...], qa)
    kvt = _dot_nt(wkvb_ref[...], kva)
    fm = _dot_nt(wfm_ref[...], hb)

    cosm, sinm = cosm_ref[...], sinm_ref[...]
    cosg, sing = cosg_ref[...], sing_ref[...]
    gmq, gmk = gmq_ref[...], gmk_ref[...]
    ggq, ggk = ggq_ref[...], ggk_ref[...]
    kr = fm[R_KR:R_KR + MLA_ROPE]
    kr_ss = jnp.sum(kr * kr, axis=0, keepdims=True)
    zpad = jnp.zeros((HEAD_PAD - MLA_QK, TM), F32)

    for hd in range(MLA_HEADS):
        q = qt[hd * HEAD_PAD:(hd + 1) * HEAD_PAD]
        r = lax.rsqrt(jnp.sum(q * q, axis=0, keepdims=True) * (1.0 / MLA_QK) + EPS)
        qn = q * r * gmq
        qr = _rope_rows(qn[MLA_NOPE:MLA_QK], cosm, sinm)
        qfull = jnp.concatenate([qn[:MLA_NOPE], qr, zpad], axis=0) * MLA_SCALE
        qtm_ref[hd] = qfull.astype(BF)

        base = hd * (MLA_NOPE + MLA_V)
        kn = kvt[base:base + MLA_NOPE]
        vtm_ref[0, hd * MLA_V:(hd + 1) * MLA_V, :] = kvt[base + MLA_NOPE:base + MLA_NOPE + MLA_V].astype(BF)
        rk = lax.rsqrt((jnp.sum(kn * kn, axis=0, keepdims=True) + kr_ss) * (1.0 / MLA_QK) + EPS)
        knn = kn * rk * gmk[:MLA_NOPE]
        krn = _rope_rows(kr * rk * gmk[MLA_NOPE:MLA_QK], cosm, sinm)
        kfull = jnp.concatenate([knn, krn, zpad], axis=0)
        km_ref[hd] = kfull.T.astype(BF)
        knm_ref[hd:hd + 1, :] = _sq_norm_as_stored(kfull)

    zhalf = jnp.zeros((GQA_HEAD_DIM, TM), F32)
    for hd in range(GQA_HEADS):
        q = fm[R_GQ + hd * GQA_HEAD_DIM:R_GQ + (hd + 1) * GQA_HEAD_DIM]
        r = lax.rsqrt(jnp.mean(q * q, axis=0, keepdims=True) + EPS)
        qn = _rope_rows(q * r * ggq, cosg, sing) * GQA_SCALE
        parts = [qn, zhalf] if hd // GQA_GROUP == 0 else [zhalf, qn]
        qtg_ref[hd] = jnp.concatenate(parts, axis=0).astype(BF)
    ks = []
    for g in range(GQA_KV_HEADS):
        k = fm[R_GK + g * GQA_HEAD_DIM:R_GK + (g + 1) * GQA_HEAD_DIM]
        r = lax.rsqrt(jnp.mean(k * k, axis=0, keepdims=True) + EPS)
        ks.append(_rope_rows(k * r * ggk, cosg, sing))
        kng_ref[g:g + 1, :] = _sq_norm_as_stored(ks[g])
    kg_ref[...] = jnp.concatenate(ks, axis=0).T.astype(BF)
    vtg_ref[0] = fm[R_GV:R_KR].astype(BF)


def _proj_call(xall, mod_l, gn, wmain, wfm, wqb, wkvb, layer, bgate, gqa, gkva, gmq, gmk, ggq, ggk,
               cosm, sinm, cosg, sing):
    def stream(i):
        return jnp.where(i == 0, 1, 0)

    in_specs = [
        pl.BlockSpec((TM, D), lambda i: (i, 0)),
        pl.BlockSpec((1, N_MOD, D), lambda i: (stream(i), 0, 0)),
        _const_spec((1, D)),
        _layer_spec((D, W_MAIN), layer),
        _layer_spec((W_FM, D), layer),
        _layer_spec((MLA_HEADS * HEAD_PAD, MLA_Q_RANK), layer),
        _layer_spec((MLA_HEADS * (MLA_NOPE + MLA_V), MLA_KV_RANK), layer),
        _const_spec((1, 3 * D)),
        _const_spec((1, MLA_Q_RANK)),
        _const_spec((1, MLA_KV_RANK)),
        _const_spec((HEAD_PAD, 1)),
        _const_spec((HEAD_PAD, 1)),
        _const_spec((GQA_HEAD_DIM, 1)),
        _const_spec((GQA_HEAD_DIM, 1)),
        pl.BlockSpec((MLA_ROPE, TM), lambda i: (0, i)),
        pl.BlockSpec((MLA_ROPE, TM), lambda i: (0, i)),
        pl.BlockSpec((GQA_HEAD_DIM, TM), lambda i: (0, i)),
        pl.BlockSpec((GQA_HEAD_DIM, TM), lambda i: (0, i)),
    ]
    out_specs = [
        pl.BlockSpec((TM, CONV_DIM), lambda i: (i, 0)),
        pl.BlockSpec((TM, 3 * D), lambda i: (i, 0)),
        pl.BlockSpec((MLA_HEADS, HEAD_PAD, TM), lambda i: (0, 0, i)),
        pl.BlockSpec((MLA_HEADS, TM, HEAD_PAD), lambda i: (0, i, 0)),
        pl.BlockSpec((1, MLA_HEADS * MLA_V, KC), lambda i: (i, 0, 0)),
        pl.BlockSpec((MLA_HEADS, TM), lambda i: (0, i)),
        pl.BlockSpec((GQA_HEADS, HEAD_PAD, TM), lambda i: (0, 0, i)),
        pl.BlockSpec((TM, HEAD_PAD), lambda i: (i, 0)),
        pl.BlockSpec((1, GQA_KV_HEADS * GQA_HEAD_DIM, KC), lambda i: (i, 0, 0)),
        pl.BlockSpec((GQA_KV_HEADS, TM), lambda i: (0, i)),
    ]
    out_shape = [
        jax.ShapeDtypeStruct((NA, CONV_DIM), F32),
        jax.ShapeDtypeStruct((NA, 3 * D), F32),
        jax.ShapeDtypeStruct((MLA_HEADS, HEAD_PAD, NA), BF),
        jax.ShapeDtypeStruct((MLA_HEADS, NA, HEAD_PAD), BF),
        jax.ShapeDtypeStruct((NT, MLA_HEADS * MLA_V, KC), BF),
        jax.ShapeDtypeStruct((MLA_HEADS, NA), F32),
        jax.ShapeDtypeStruct((GQA_HEADS, HEAD_PAD, NA), BF),
        jax.ShapeDtypeStruct((NA, HEAD_PAD), BF),
        jax.ShapeDtypeStruct((NT, GQA_KV_HEADS * GQA_HEAD_DIM, KC), BF),
        jax.ShapeDtypeStruct((GQA_KV_HEADS, NA), F32),
    ]
    return pl.pallas_call(
        _proj_kernel,
        grid=(NT,),
        in_specs=in_specs,
        out_specs=out_specs,
        out_shape=out_shape,
        compiler_params=pltpu.CompilerParams(dimension_semantics=("arbitrary",),
                                             vmem_limit_bytes=VMEM_LIMIT),
        name="proj",
    )(xall, mod_l, gn, wmain, wfm, wqb, wkvb, bgate, gqa, gkva, gmq, gmk, ggq, ggk,
      cosm, sinm, cosg, sing)


def _attn_kernel(q_ref, k_ref, vt_ref, kn_ref, o_ref, *, k_per_head, v_per_head):
    j = pl.program_id(1)

    def k_head(hh):
        return hh if k_per_head else 0

    def v_rows(hh):
        v0 = hh * MLA_V if v_per_head else 0
        return slice(v0, v0 + MLA_V)

    def write(outs):
        o_ref[...] = jnp.concatenate(outs, axis=0).T.astype(o_ref.dtype)

    def scores(hh, c):
        start = pl.multiple_of(c * KC, KC)
        return _dot(k_ref[k_head(hh), pl.ds(start, KC), :], q_ref[hh])

    def pv(hh, c, p):
        return _dot(vt_ref[c, v_rows(hh), :], p.astype(BF))

    def online(hh, n_chunks):
        s = scores(hh, 0)
        m = jnp.max(s, axis=0, keepdims=True)
        p = jnp.exp2(s - m)
        carry = (m, jnp.sum(p, axis=0, keepdims=True), pv(hh, 0, p))

        def body(c, carry):
            m, den, acc = carry
            s = scores(hh, c)
            m_new = jnp.maximum(m, jnp.max(s, axis=0, keepdims=True))
            alpha = jnp.exp2(m - m_new)
            p = jnp.exp2(s - m_new)
            return m_new, alpha * den + jnp.sum(p, axis=0, keepdims=True), alpha * acc + pv(hh, c, p)

        _, den, acc = lax.fori_loop(1, n_chunks, body, carry)
        return acc / den

    @pl.when(j == 0)
    def _():
        write([online(hh, 1) for hh in range(HPS)])

    @pl.when(j > 0)
    def _():
        shift = []
        for hh in range(HPS):
            qf = q_ref[hh].astype(F32)
            qn2 = jnp.sum(qf * qf, axis=0, keepdims=True)
            kn2 = jnp.max(jnp.max(kn_ref[k_head(hh)], axis=0, keepdims=True), axis=1, keepdims=True)
            shift.append(jnp.sqrt(qn2 * kn2))
        acc = [jnp.zeros((MLA_V, MQ), F32)] * HPS
        den = [jnp.zeros((1, MQ), F32)] * HPS
        p_prev = [None] * HPS
        rows = QK_GROUP * KC
        for g in range(NT // QK_GROUP + 1):
            p_new = [None] * HPS
            for hh in range(HPS):
                if g < NT // QK_GROUP:
                    s = _dot(k_ref[k_head(hh), g * rows:(g + 1) * rows, :], q_ref[hh])
                    e = jnp.exp2(s - shift[hh])
                    den[hh] = den[hh] + jnp.sum(e, axis=0, keepdims=True)
                    p_new[hh] = e.astype(BF)
                if g >= 1:
                    for u in range(QK_GROUP):
                        acc[hh] = acc[hh] + _dot(vt_ref[(g - 1) * QK_GROUP + u, v_rows(hh), :],
                                                 p_prev[hh][u * KC:(u + 1) * KC])
            p_prev = p_new
        write([a / d for a, d in zip(acc, den)])
        den_min = jnp.min(functools.reduce(jnp.minimum, den))

        @pl.when(jnp.logical_not(den_min >= DEN_FLOOR))
        def _():
            write([online(hh, NT) for hh in range(HPS)])


def _attn_call(qt, k, vt, kn, *, k_per_head, v_per_head, name):
    if k_per_head:
        k_spec = pl.BlockSpec((HPS, NA, HEAD_PAD), lambda p, j: (p, 0, 0))
        kn_spec = pl.BlockSpec((HPS, NA // LANES, LANES), lambda p, j: (p, 0, 0))
    else:
        k_spec = pl.BlockSpec((1, NA, HEAD_PAD), lambda p, j: (0, 0, 0))
        kn_spec = pl.BlockSpec((1, NA // LANES, LANES), lambda p, j: (p * HPS // GQA_GROUP, 0, 0))
    if v_per_head:
        v_spec = pl.BlockSpec((NT, HPS * MLA_V, KC), lambda p, j: (0, p, 0))
    else:
        v_spec = pl.BlockSpec((NT, GQA_HEAD_DIM, KC), lambda p, j: (0, p * HPS // GQA_GROUP, 0))
    return pl.pallas_call(
        partial(_attn_kernel, k_per_head=k_per_head, v_per_head=v_per_head),
        grid=(MLA_HEADS // HPS, NA // MQ),
        in_specs=[pl.BlockSpec((HPS, HEAD_PAD, MQ), lambda p, j: (p, 0, j)), k_spec, v_spec, kn_spec],
        out_specs=pl.BlockSpec((MQ, HPS * MLA_V), lambda p, j: (j, p)),
        out_shape=jax.ShapeDtypeStruct((NA, 8 * MLA_V), BF),
        compiler_params=pltpu.CompilerParams(dimension_semantics=("arbitrary", "arbitrary"),
                                             vmem_limit_bytes=VMEM_LIMIT),
        name=name,
    )(qt, k, vt, kn)


def _merge_kernel(x_ref, yp_ref, yc_ref, yn_ref, gates_ref, om_ref, og_ref, mod_ref,
                  cw_ref, cb_ref, lng_ref, lnb_ref, wco_ref, wmo_ref, wgo_ref, wout_ref,
                  o_ref, buf_ref, sh_ref):
    i = pl.program_id(0)
    left_ok = i >= 2
    right_ok = jnp.logical_and(i >= 1, i < NT - 1)
    buf_ref[0:HALO_Y, :] = jnp.where(left_ok, yp_ref[...], 0.0)
    buf_ref[HALO_Y:HALO_Y + TM, :] = yc_ref[...]
    buf_ref[HALO_Y + TM:HALO_Y + TM + HALO_Y, :] = jnp.where(right_ok, yn_ref[...], 0.0)

    for r in range(1, SUBLANES):
        sh_ref[r - 1] = buf_ref[r:r + CONV_SPAN, :]
    off = HALO_Y - CONV_WIDTH // 2
    acc = jnp.zeros((TM, CONV_DIM), F32) + cb_ref[...]
    for k in range(CONV_WIDTH):
        a, r = divmod(off + k, SUBLANES)
        src = buf_ref if r == 0 else sh_ref.at[r - 1]
        acc = acc + src[a * SUBLANES:a * SUBLANES + TM, :] * cw_ref[k:k + 1, :]
    mu = jnp.mean(acc, axis=-1, keepdims=True)
    xc = acc - mu
    var = jnp.mean(xc * xc, axis=-1, keepdims=True)
    ln = xc * lax.rsqrt(var + EPS) * lng_ref[...] + lnb_ref[...]
    act = (ln * _sigmoid(ln)).astype(BF)
    br_conv = _dot(act, wco_ref[...])
    br_mla = _dot(om_ref[...], wmo_ref[...])
    br_gqa = _dot(og_ref[...], wgo_ref[...])
    merged = (gates_ref[:, 0:D] * br_conv + gates_ref[:, D:2 * D] * br_mla
              + gates_ref[:, 2 * D:3 * D] * br_gqa)
    res = _dot(merged.astype(BF), wout_ref[...])
    o_ref[...] = x_ref[...] + mod_ref[0, 2:3, :] * res


def _merge_call(xall, y, gates, om, og, mod_l, cw, cb, lng, lnb, wco, wmo, wgo, wout, layer):
    ny = NA // HALO_Y
    per = TM // HALO_Y
    in_specs = [
        pl.BlockSpec((TM, D), lambda i: (i, 0)),
        pl.BlockSpec((HALO_Y, CONV_DIM), lambda i: (jnp.maximum(i * per - 1, 0), 0)),
        pl.BlockSpec((TM, CONV_DIM), lambda i: (i, 0)),
        pl.BlockSpec((HALO_Y, CONV_DIM), lambda i: (jnp.minimum((i + 1) * per, ny - 1), 0)),
        pl.BlockSpec((TM, 3 * D), lambda i: (i, 0)),
        pl.BlockSpec((TM, 8 * MLA_V), lambda i: (i, 0)),
        pl.BlockSpec((TM, 8 * MLA_V), lambda i: (i, 0)),
        pl.BlockSpec((1, N_MOD, D), lambda i: (jnp.where(i == 0, 1, 0), 0, 0)),
        _const_spec((CONV_WIDTH, CONV_DIM)),
        _const_spec((1, CONV_DIM)),
        _const_spec((1, CONV_DIM)),
        _const_spec((1, CONV_DIM)),
        _layer_spec((CONV_DIM, D), layer),
        _layer_spec((8 * MLA_V, D), layer),
        _layer_spec((8 * GQA_HEAD_DIM, D), layer),
        _layer_spec((D, D), layer),
    ]
    return pl.pallas_call(
        _merge_kernel,
        grid=(NT,),
        in_specs=in_specs,
        out_specs=pl.BlockSpec((TM, D), lambda i: (i, 0)),
        out_shape=jax.ShapeDtypeStruct((NA, D), F32),
        scratch_shapes=[pltpu.VMEM((TM + 2 * HALO_Y, CONV_DIM), F32),
                        pltpu.VMEM((SUBLANES - 1, CONV_SPAN, CONV_DIM), F32)],
        compiler_params=pltpu.CompilerParams(dimension_semantics=("arbitrary",),
                                             vmem_limit_bytes=VMEM_LIMIT),
        name="merge",
    )(xall, y, y, y, gates, om, og, mod_l, cw, cb, lng, lnb, wco, wmo, wgo, wout)


FF_CHUNK = D_FF


def _ffn_kernel(xp_ref, xc_ref, xn_ref, mod_ref, gn_ref, wup_ref, dw_ref, db_ref, wdown_ref,
                o_ref, ua_ref, ug_ref):
    i = pl.program_id(0)
    left_ok = i >= 2
    right_ok = jnp.logical_and(i >= 1, i < NT - 1)
    xc = xc_ref[...]
    xh = jnp.concatenate([xp_ref[...], xc, xn_ref[...]], axis=0)
    shift = mod_ref[0, 3:4, :]
    scale = mod_ref[0, 4:5, :]
    h = xh * lax.rsqrt(jnp.mean(xh * xh, axis=-1, keepdims=True) + EPS) * gn_ref[...]
    hb = (h * (1.0 + scale) + shift).astype(BF)

    rows = TM + 2 * HALO_X
    row_id = lax.broadcasted_iota(jnp.int32, (rows, 1), 0)
    keep = jnp.logical_and(jnp.logical_or(row_id >= HALO_X, left_ok),
                           jnp.logical_or(row_id < HALO_X + TM, right_ok))

    def conv3(u_ref, col0, width):
        w = dw_ref[:, col0:col0 + width]
        return (u_ref[HALO_X - 1:HALO_X - 1 + TM, :] * w[0:1]
                + u_ref[HALO_X:HALO_X + TM, :] * w[1:2]
                + u_ref[HALO_X + 1:HALO_X + 1 + TM, :] * w[2:3]
                + db_ref[:, col0:col0 + width])

    acc = jnp.zeros((TM, D), F32)
    for c in range(D_FF // FF_CHUNK):
        a0 = c * FF_CHUNK
        g0 = D_FF + c * FF_CHUNK
        ug_ref[...] = jnp.where(keep, _dot(hb, wup_ref[:, g0:g0 + FF_CHUNK]), 0.0)
        ua_ref[...] = jnp.where(keep, _dot(hb, wup_ref[:, a0:a0 + FF_CHUNK]), 0.0)
        g = conv3(ug_ref, g0, FF_CHUNK)
        gate = g * _sigmoid(g)
        a = conv3(ua_ref, a0, FF_CHUNK)
        act = (gate * a).astype(BF)
        acc = acc + _dot(act, wdown_ref[a0:a0 + FF_CHUNK, :])
    o_ref[...] = xc + mod_ref[0, 5:6, :] * acc


def _ffn_call(x1, mod_l, gn, wup, dw, db, wdown, layer):
    nx = NA // HALO_X
    per = TM // HALO_X
    in_specs = [
        pl.BlockSpec((HALO_X, D), lambda i: (jnp.maximum(i * per - 1, 0), 0)),
        pl.BlockSpec((TM, D), lambda i: (i, 0)),
        pl.BlockSpec((HALO_X, D), lambda i: (jnp.minimum((i + 1) * per, nx - 1), 0)),
        pl.BlockSpec((1, N_MOD, D), lambda i: (jnp.where(i == 0, 1, 0), 0, 0)),
        _const_spec((1, D)),
        _layer_spec((D, 2 * D_FF), layer),
        _const_spec((3, 2 * D_FF)),
        _const_spec((1, 2 * D_FF)),
        _layer_spec((D_FF, D), layer),
    ]
    return pl.pallas_call(
        _ffn_kernel,
        grid=(NT,),
        in_specs=in_specs,
        out_specs=pl.BlockSpec((TM, D), lambda i: (i, 0)),
        out_shape=jax.ShapeDtypeStruct((NA, D), F32),
        scratch_shapes=[pltpu.VMEM((TM + 2 * HALO_X, FF_CHUNK), F32),
                        pltpu.VMEM((TM + 2 * HALO_X, FF_CHUNK), F32)],
        compiler_params=pltpu.CompilerParams(dimension_semantics=("arbitrary",),
                                             vmem_limit_bytes=VMEM_LIMIT),
        name="ffn",
    )(x1, x1, x1, mod_l, gn, wup, dw, db, wdown)


def _rope_tables(rot_dim):
    n_freq = rot_dim // 4
    inv = 1.0 / (ROPE_THETA ** (jnp.arange(n_freq, dtype=F32) / n_freq))
    pos = jnp.arange(SEQ, dtype=jnp.int32)
    ang_r = (pos // GRID_W).astype(F32)[None, :] * inv[:, None]
    ang_c = (pos % GRID_W).astype(F32)[None, :] * inv[:, None]
    ang = jnp.concatenate([ang_r, ang_r, ang_c, ang_c], axis=0)
    cos = jnp.concatenate([jnp.ones((rot_dim, CTX), F32), jnp.cos(ang)], axis=1)
    sin = jnp.concatenate([jnp.zeros((rot_dim, CTX), F32), jnp.sin(ang)], axis=1)
    return cos, sin


def _pad_col(g, n):
    return jnp.pad(g, (0, n - g.shape[0])).reshape(n, 1)


def kernel(x, c, ctx, c_ctx, w_mod, b_mod, g_norm1, g_norm2, w_in, b_gate, conv_dw_w, conv_dw_b, conv_ln_g, conv_ln_b, w_conv_out, g_q_a, w_q_b, g_kv_a, w_kv_b, g_mla_q, g_mla_k, w_mla_o, g_gqa_q, g_gqa_k, w_gqa_o, w_out, w_up, ffn_dw_w, ffn_dw_b, w_down):
    assert x.shape == (1, SEQ, D) and ctx.shape == (1, CTX, D)
    xall = jnp.concatenate([ctx[0], x[0]], axis=0)
    cond_t = jnp.stack([c[0], c_ctx], axis=1)
    mod = _mod_call(cond_t, w_mod, b_mod).reshape(DEPTH, 2, N_MOD, D)
    cosm, sinm = _rope_tables(MLA_ROPE)
    cosg, sing = _rope_tables(GQA_HEAD_DIM)

    o_qa = 2 * CONV_DIM
    o_kva = o_qa + MLA_Q_RANK
    o_kr = o_kva + MLA_KV_RANK
    o_gq = o_kr + MLA_ROPE
    o_gk = o_gq + GQA_HEADS * GQA_HEAD_DIM
    o_gv = o_gk + GQA_KV_HEADS * GQA_HEAD_DIM
    o_gate = o_gv + GQA_KV_HEADS * GQA_HEAD_DIM

    wco_b, wmo_b, wgo_b, wout_b = (w.astype(BF) for w in (w_conv_out, w_mla_o, w_gqa_o, w_out))
    wup_b, wdown_b = w_up.astype(BF), w_down.astype(BF)
    wi = w_in
    wmain = jnp.concatenate([wi[:, :, :o_qa], wi[:, :, o_gate:], wi[:, :, o_qa:o_kr]], axis=2).astype(BF)
    wfm = jnp.concatenate([wi[:, :, o_gq:o_gate], wi[:, :, o_kr:o_gq]], axis=2).transpose(0, 2, 1).astype(BF)
    wqb = jnp.pad(w_q_b.reshape(DEPTH, MLA_Q_RANK, MLA_HEADS, MLA_QK),
                  ((0, 0), (0, 0), (0, 0), (0, HEAD_PAD - MLA_QK)))
    wqb = wqb.reshape(DEPTH, MLA_Q_RANK, MLA_HEADS * HEAD_PAD).transpose(0, 2, 1).astype(BF)
    wkvb = w_kv_b.transpose(0, 2, 1).astype(BF)

    for l in range(DEPTH):
        mod_l = mod[l]

        y, gates, qtm, km, vtm, knm, qtg, kg, vtg, kng = _proj_call(
            xall, mod_l, g_norm1[l].reshape(1, D), wmain, wfm, wqb, wkvb, l, b_gate[l].reshape(1, 3 * D),
            g_q_a[l].reshape(1, -1), g_kv_a[l].reshape(1, -1),
            _pad_col(g_mla_q[l], HEAD_PAD), _pad_col(g_mla_k[l], HEAD_PAD),
            g_gqa_q[l].reshape(-1, 1), g_gqa_k[l].reshape(-1, 1), cosm, sinm, cosg, sing)
        om = _attn_call(qtm, km, vtm, knm.reshape(MLA_HEADS, NA // LANES, LANES),
                        k_per_head=True, v_per_head=True, name="attn_mla")
        og = _attn_call(qtg, kg.reshape(1, NA, HEAD_PAD), vtg, kng.reshape(GQA_KV_HEADS, NA // LANES, LANES),
                        k_per_head=False, v_per_head=False, name="attn_gqa")
        x1 = _merge_call(xall, y, gates, om, og, mod_l, conv_dw_w[l], conv_dw_b[l].reshape(1, -1),
                         conv_ln_g[l].reshape(1, -1), conv_ln_b[l].reshape(1, -1),
                         wco_b, wmo_b, wgo_b, wout_b, l)
        xall = _ffn_call(x1, mod_l, g_norm2[l].reshape(1, D), wup_b, ffn_dw_w[l],
                         ffn_dw_b[l].reshape(1, -1), wdown_b, l)
    return xall[CTX:][None]
```
